```python
import math
import jax, jax.numpy as jnp
from jax import lax
import numpy as np

D_MODEL = 1024
BATCH = 8
SEQ = 4096
DEPTH = 2

N_A_LAYERS = (DEPTH + 1) // 2
N_B_LAYERS = DEPTH - N_A_LAYERS
HEAD_DIM = 64
MEM_HEADS = 4
MEM_WIDTH = MEM_HEADS * HEAD_DIM
PRIMARY_WIDTH = D_MODEL - MEM_WIDTH
B_HEADS = PRIMARY_WIDTH // HEAD_DIM
CONV_CH = PRIMARY_WIDTH
CONV_WIDTH = 31
D_FF = 256 * ((8 * D_MODEL // 3 + 255) // 256)
MOBA_BLOCK = 256
MOBA_TOPK = 3
Q_CHUNK = 16
ROPE_THETA = 500000.0
ROPE_DIM = HEAD_DIM // 4
EPS = 1e-6

kernel_name = 'conformer_conv_moba_yoco_hybrid'


def rms_norm(x, g):
    xf = x.astype(jnp.float32)
    y = xf * lax.rsqrt(jnp.mean(xf * xf, axis=-1, keepdims=True) + EPS)
    return (y * g.astype(jnp.float32)).astype(x.dtype)


def layer_norm(x, g, b):
    xf = x.astype(jnp.float32)
    xc = xf - jnp.mean(xf, axis=-1, keepdims=True)
    y = xc * lax.rsqrt(jnp.mean(xc * xc, axis=-1, keepdims=True) + EPS)
    return (y * g.astype(jnp.float32) + b.astype(jnp.float32)).astype(x.dtype)


def swiglu(x, w_gate, w_up, w_down):
    return (jax.nn.silu(x @ w_gate) * (x @ w_up)) @ w_down


def partial_rope(x, positions):
    half = ROPE_DIM // 2
    inv_freq = 1.0 / (ROPE_THETA ** (jnp.arange(0, ROPE_DIM, 2, dtype=jnp.float32) / ROPE_DIM))
    ang = positions.astype(jnp.float32)[..., None] * inv_freq
    cos = jnp.cos(ang)[:, :, None, :]
    sin = jnp.sin(ang)[:, :, None, :]
    xr = x[..., :ROPE_DIM].astype(jnp.float32)
    x1, x2 = xr[..., :half], xr[..., half:]
    rot = jnp.concatenate([x1 * cos - x2 * sin, x2 * cos + x1 * sin], axis=-1)
    return jnp.concatenate([rot.astype(x.dtype), x[..., ROPE_DIM:]], axis=-1)


def conformer_conv(u, dw_kernel, dw_bias, ln_g, ln_b):
    a, gate = jnp.split(u, 2, axis=-1)
    h = a * jax.nn.sigmoid(gate)
    h = lax.conv_general_dilated(h, dw_kernel.astype(h.dtype), window_strides=(1,),
                                 padding=[(CONV_WIDTH - 1, 0)],
                                 dimension_numbers=('NWC', 'WIO', 'NWC'),
                                 feature_group_count=CONV_CH) + dw_bias
    return jax.nn.silu(layer_norm(h, ln_g, ln_b))


def memory_attention(q, mem_n, w_mem_kv, q_g, k_g):
    B, S = q.shape[:2]
    M = mem_n.shape[1]
    k, v = jnp.split(mem_n @ w_mem_kv, 2, axis=-1)
    k = rms_norm(k.reshape(B, M, MEM_HEADS, HEAD_DIM), k_g)
    v = v.reshape(B, M, MEM_HEADS, HEAD_DIM)
    q = rms_norm(q.reshape(B, S, MEM_HEADS, HEAD_DIM), q_g)
    logits = jnp.einsum('bshd,bmhd->bhsm', q, k).astype(jnp.float32) * (HEAD_DIM ** -0.5)
    p = jax.nn.softmax(logits, axis=-1).astype(v.dtype)
    return jnp.einsum('bhsm,bmhd->bshd', p, v).reshape(B, S, MEM_WIDTH)


def shared_moba_kv(h, positions, kv_norm_g, w_kv, k_norm_g):
    B, S, _ = h.shape
    k, v = jnp.split(rms_norm(h, kv_norm_g) @ w_kv, 2, axis=-1)
    k = partial_rope(rms_norm(k.reshape(B, S, B_HEADS, HEAD_DIM), k_norm_g), positions)
    v = v.reshape(B, S, B_HEADS, HEAD_DIM)
    n_blocks = -(-S // MOBA_BLOCK)
    pad = n_blocks * MOBA_BLOCK - S

    def to_blocks(t):
        t = jnp.pad(t, ((0, 0), (0, pad), (0, 0), (0, 0)))
        return t.reshape(B, n_blocks, MOBA_BLOCK, B_HEADS, HEAD_DIM).transpose(0, 3, 1, 2, 4)

    k_blocks = to_blocks(k)
    v_blocks = to_blocks(v)
    k_mean = jnp.mean(k_blocks.astype(jnp.float32), axis=3).astype(k_blocks.dtype)
    return k_blocks, v_blocks, k_mean


def moba_attention(q, k_blocks, v_blocks, k_mean):
    B, S, H, Dh = q.shape
    n_blocks = k_blocks.shape[2]
    topk = min(MOBA_TOPK, n_blocks)
    n_chunks = S // Q_CHUNK
    scale = Dh ** -0.5
    q_chunks = q.reshape(B, n_chunks, Q_CHUNK, H, Dh).transpose(1, 0, 3, 2, 4)
    b_idx = jnp.arange(B)[:, None, None, None]
    h_idx = jnp.arange(H)[None, :, None, None]

    def chunk_attend(args):
        q_c, c = args
        start = c * Q_CHUNK
        own = start // MOBA_BLOCK
        gate = jnp.einsum('bhqd,bhnd->bhqn', q_c, k_mean).astype(jnp.float32)
        gate = jnp.where(jnp.arange(n_blocks) < own, gate, -jnp.inf)
        _, sel = lax.top_k(gate, topk)
        valid = jnp.arange(topk) < own
        k_sel = k_blocks[b_idx, h_idx, sel]
        v_sel = v_blocks[b_idx, h_idx, sel]
        k_own = lax.dynamic_index_in_dim(k_blocks, own, axis=2, keepdims=False)
        v_own = lax.dynamic_index_in_dim(v_blocks, own, axis=2, keepdims=False)
        s_sel = jnp.einsum('bhqd,bhqnjd->bhqnj', q_c, k_sel).astype(jnp.float32) * scale
        s_sel = jnp.where(valid[:, None], s_sel, -jnp.inf)
        s_own = jnp.einsum('bhqd,bhjd->bhqj', q_c, k_own).astype(jnp.float32) * scale
        q_pos = start + jnp.arange(Q_CHUNK)
        k_pos = own * MOBA_BLOCK + jnp.arange(MOBA_BLOCK)
        s_own = jnp.where(k_pos[None, :] <= q_pos[:, None], s_own, -jnp.inf)
        logits = jnp.concatenate([s_sel.reshape(B, H, Q_CHUNK, topk * MOBA_BLOCK), s_own], axis=-1)
        p = jax.nn.softmax(logits, axis=-1).astype(v_blocks.dtype)
        p_sel = p[..., :topk * MOBA_BLOCK].reshape(B, H, Q_CHUNK, topk, MOBA_BLOCK)
        p_own = p[..., topk * MOBA_BLOCK:]
        return (jnp.einsum('bhqnj,bhqnjd->bhqd', p_sel, v_sel)
                + jnp.einsum('bhqj,bhjd->bhqd', p_own, v_own))

    out = lax.map(chunk_attend, (q_chunks, jnp.arange(n_chunks)))
    return out.transpose(1, 0, 3, 2, 4).reshape(B, S, H * Dh)


def setup_inputs(seed: int = 0) -> dict:
    key = jax.random.key(seed)
    it = iter(jax.random.split(key, 64))

    def w(shape, fan_in):
        return jax.random.normal(next(it), shape, jnp.float32) * fan_in ** -0.5

    def gain(shape):
        return 1.0 + 0.01 * jax.random.normal(next(it), shape, jnp.float32)

    def bias(shape):
        return 0.01 * jax.random.normal(next(it), shape, jnp.float32)

    x = jax.random.normal(next(it), (BATCH, SEQ, D_MODEL), jnp.float32)
    mem = jax.random.normal(next(it), (BATCH, 256, D_MODEL), jnp.float32)
    offsets = jax.random.randint(next(it), (BATCH, 1), 0, 8192, dtype=jnp.int32)
    positions = (offsets + jnp.arange(SEQ, dtype=jnp.int32)[None, :]).astype(jnp.int32)
    return {
        'x': x, 'mem': mem, 'positions': positions,
        'ffn1_norm_g': gain((DEPTH, D_MODEL)),
        'ffn1_w_gate': w((DEPTH, D_MODEL, D_FF), D_MODEL),
        'ffn1_w_up': w((DEPTH, D_MODEL, D_FF), D_MODEL),
        'ffn1_w_down': w((DEPTH, D_FF, D_MODEL), D_FF),
        'mix_norm_g': gain((DEPTH, D_MODEL)),
        'mem_norm_g': gain((DEPTH, D_MODEL)),
        'w_mem_kv': w((DEPTH, D_MODEL, 2 * MEM_WIDTH), D_MODEL),
        'mem_q_norm_g': gain((DEPTH, HEAD_DIM)),
        'mem_k_norm_g': gain((DEPTH, HEAD_DIM)),
        'w_o': w((DEPTH, PRIMARY_WIDTH + MEM_WIDTH, D_MODEL), PRIMARY_WIDTH + MEM_WIDTH),
        'ffn2_norm_g': gain((DEPTH, D_MODEL)),
        'ffn2_w_gate': w((DEPTH, D_MODEL, D_FF), D_MODEL),
        'ffn2_w_up': w((DEPTH, D_MODEL, D_FF), D_MODEL),
        'ffn2_w_down': w((DEPTH, D_FF, D_MODEL), D_FF),
        'a_w_in': w((N_A_LAYERS, D_MODEL, 2 * CONV_CH + MEM_WIDTH), D_MODEL),
        'a_dw_kernel': w((N_A_LAYERS, CONV_WIDTH, 1, CONV_CH), CONV_WIDTH),
        'a_dw_bias': bias((N_A_LAYERS, CONV_CH)),
        'a_ln_g': gain((N_A_LAYERS, CONV_CH)),
        'a_ln_b': bias((N_A_LAYERS, CONV_CH)),
        'kv_norm_g': gain((D_MODEL,)),
        'w_kv': w((D_MODEL, 2 * PRIMARY_WIDTH), D_MODEL),
        'k_norm_g': gain((HEAD_DIM,)),
        'b_w_in': w((N_B_LAYERS, D_MODEL, PRIMARY_WIDTH + MEM_WIDTH), D_MODEL),
        'b_q_norm_g': gain((N_B_LAYERS, HEAD_DIM)),
    }


def reference(x, mem, positions, ffn1_norm_g, ffn1_w_gate, ffn1_w_up, ffn1_w_down,
              mix_norm_g, mem_norm_g, w_mem_kv, mem_q_norm_g, mem_k_norm_g, w_o,
              ffn2_norm_g, ffn2_w_gate, ffn2_w_up, ffn2_w_down,
              a_w_in, a_dw_kernel, a_dw_bias, a_ln_g, a_ln_b,
              kv_norm_g, w_kv, k_norm_g, b_w_in, b_q_norm_g):
    B, S, _ = x.shape
    h = x
    shared_kv = None
    for layer in range(DEPTH):
        if layer == N_A_LAYERS:
            shared_kv = shared_moba_kv(h, positions, kv_norm_g, w_kv, k_norm_g)
        h = h + 0.5 * swiglu(rms_norm(h, ffn1_norm_g[layer]), ffn1_w_gate[layer],
                             ffn1_w_up[layer], ffn1_w_down[layer])
        hn = rms_norm(h, mix_norm_g[layer])
        if layer < N_A_LAYERS:
            u = hn @ a_w_in[layer]
            primary = conformer_conv(u[..., :2 * CONV_CH], a_dw_kernel[layer], a_dw_bias[layer],
                                     a_ln_g[layer], a_ln_b[layer])
            q_mem = u[..., 2 * CONV_CH:]
        else:
            j = layer - N_A_LAYERS
            u = hn @ b_w_in[j]
            q = rms_norm(u[..., :PRIMARY_WIDTH].reshape(B, S, B_HEADS, HEAD_DIM), b_q_norm_g[j])
            k_blocks, v_blocks, k_mean = shared_kv
            primary = moba_attention(partial_rope(q, positions), k_blocks, v_blocks, k_mean)
            q_mem = u[..., PRIMARY_WIDTH:]
        mem_out = memory_attention(q_mem, rms_norm(mem, mem_norm_g[layer]), w_mem_kv[layer],
                                   mem_q_norm_g[layer], mem_k_norm_g[layer])
        h = h + jnp.concatenate([primary, mem_out], axis=-1) @ w_o[layer]
        h = h + 0.5 * swiglu(rms_norm(h, ffn2_norm_g[layer]), ffn2_w_gate[layer],
                             ffn2_w_up[layer], ffn2_w_down[layer])
    return h
```

```python
import functools

import numpy as np
import jax
import jax.numpy as jnp
from jax import lax
from jax.experimental import pallas as pl
from jax.experimental.pallas import tpu as pltpu

F32 = jnp.float32
BF16 = jnp.bfloat16

D_MODEL = 1024
HEAD_DIM = 64
MEM_HEADS = 4
MEM_WIDTH = MEM_HEADS * HEAD_DIM
PRIMARY_WIDTH = D_MODEL - MEM_WIDTH
B_HEADS = PRIMARY_WIDTH // HEAD_DIM
CONV_CH = PRIMARY_WIDTH
CONV_WIDTH = 31
MOBA_BLOCK = 256
MOBA_TOPK = 3
ROPE_THETA = 500000.0
ROPE_DIM = HEAD_DIM // 4
EPS = 1e-6
SCALE = HEAD_DIM ** -0.5

LANES = 128
SUBLANES = 8
HALO = 32
CONV_ROWS = 32
MASKED = -1e30
MIB = 1024 * 1024


def _params(semantics, vmem_mib):
    return pltpu.CompilerParams(dimension_semantics=semantics,
                                vmem_limit_bytes=vmem_mib * MIB)


def _rms(x, g):
    ms = jnp.mean(x * x, axis=-1, keepdims=True)
    return x * lax.rsqrt(ms + EPS) * g


def _head_rms(x, g, seg_mean):
    ms = jnp.dot((x * x).astype(BF16), seg_mean, preferred_element_type=F32)
    return x * lax.rsqrt(ms + EPS) * g


def _seg_mean_matrix(width):
    idx = np.arange(width) // HEAD_DIM
    return jnp.asarray((idx[:, None] == idx[None, :]).astype(np.float32) / HEAD_DIM, dtype=BF16)


def _rope(x, cos, sin):
    lane = lax.broadcasted_iota(jnp.int32, (1, LANES), 1) % HEAD_DIM
    first_half = lane < (ROPE_DIM // 2)
    outs = []
    for c in range(x.shape[1] // LANES):
        xc = x[:, c * LANES:(c + 1) * LANES]
        partner = jnp.where(first_half,
                            pltpu.roll(xc, LANES - ROPE_DIM // 2, 1),
                            pltpu.roll(xc, ROPE_DIM // 2, 1))
        outs.append(xc * cos + partner * sin)
    return jnp.concatenate(outs, axis=1)


def _mem_attention(qm, kt, v, qg, seg_mean):
    qn = _head_rms(qm, qg, seg_mean) * SCALE
    lane_head = lax.broadcasted_iota(jnp.int32, (1, MEM_WIDTH), 1) // HEAD_DIM
    out = jnp.zeros(qm.shape, F32)
    for h in range(MEM_HEADS):
        qh = jnp.where(lane_head == h, qn, 0.0).astype(BF16)
        s = jnp.dot(qh, kt, preferred_element_type=F32)
        m = jnp.max(s, axis=-1, keepdims=True)
        p = jnp.exp(s - m)
        l = jnp.sum(p, axis=-1, keepdims=True)
        vh = jnp.where(lane_head == h, v, jnp.zeros_like(v))
        out = out + jnp.dot(p.astype(BF16), vh, preferred_element_type=F32) / l
    return out


def _ffn_body(nj, x_ref, g_ref, wg_ref, wu_ref, wd_ref, o_ref, xn_ref, acc_ref):
    j = pl.program_id(1)

    @pl.when(j == 0)
    def _():
        xn_ref[...] = _rms(x_ref[...], g_ref[...]).astype(BF16)
        acc_ref[...] = jnp.zeros_like(acc_ref)

    xn = xn_ref[...]
    gate = jnp.dot(xn, wg_ref[...], preferred_element_type=F32)
    up = jnp.dot(xn, wu_ref[...], preferred_element_type=F32)
    hmid = (gate * jax.nn.sigmoid(gate) * up).astype(BF16)
    acc_ref[...] += jnp.dot(hmid, wd_ref[...], preferred_element_type=F32)

    @pl.when(j == nj - 1)
    def _():
        o_ref[...] = x_ref[...] + 0.5 * acc_ref[...]


def _ffn(h2d, g, wg, wu, wd, *, tm=512, tf=1408):
    t, d = h2d.shape
    f = wg.shape[1]
    nj = f // tf
    return pl.pallas_call(
        functools.partial(_ffn_body, nj),
        grid=(t // tm, nj),
        in_specs=[
            pl.BlockSpec((tm, d), lambda i, j: (i, 0)),
            pl.BlockSpec((1, d), lambda i, j: (0, 0)),
            pl.BlockSpec((d, tf), lambda i, j: (0, j)),
            pl.BlockSpec((d, tf), lambda i, j: (0, j)),
            pl.BlockSpec((tf, d), lambda i, j: (j, 0)),
        ],
        out_specs=pl.BlockSpec((tm, d), lambda i, j: (i, 0)),
        out_shape=jax.ShapeDtypeStruct((t, d), F32),
        scratch_shapes=[pltpu.VMEM((tm, d), BF16), pltpu.VMEM((tm, d), F32)],
        compiler_params=_params(("arbitrary", "arbitrary"), 48),
        name="ffn",
    )(h2d, g, wg, wu, wd)


def _memkv_body(mem_ref, g_ref, w_ref, kg_ref, seg_ref, kt_ref, v_ref):
    mn = _rms(mem_ref[...], g_ref[...]).astype(BF16)
    kv = jnp.dot(mn, w_ref[...], preferred_element_type=F32)
    k = _head_rms(kv[:, :MEM_WIDTH], kg_ref[...], seg_ref[...])
    kt_ref[...] = k.T.astype(BF16)
    v_ref[...] = kv[:, MEM_WIDTH:].astype(BF16)


def _memkv(mem, g, w, kg, seg):
    b, m, d = mem.shape
    return pl.pallas_call(
        _memkv_body,
        grid=(b,),
        in_specs=[
            pl.BlockSpec((None, m, d), lambda i: (i, 0, 0)),
            pl.BlockSpec((1, d), lambda i: (0, 0)),
            pl.BlockSpec((d, 2 * MEM_WIDTH), lambda i: (0, 0)),
            pl.BlockSpec((1, MEM_WIDTH), lambda i: (0, 0)),
            pl.BlockSpec((MEM_WIDTH, MEM_WIDTH), lambda i: (0, 0)),
        ],
        out_specs=[
            pl.BlockSpec((None, MEM_WIDTH, m), lambda i: (i, 0, 0)),
            pl.BlockSpec((None, m, MEM_WIDTH), lambda i: (i, 0, 0)),
        ],
        out_shape=[jax.ShapeDtypeStruct((b, MEM_WIDTH, m), BF16),
                   jax.ShapeDtypeStruct((b, m, MEM_WIDTH), BF16)],
        compiler_params=_params(("arbitrary",), 32),
        name="memkv",
    )(mem, g, w, kg, seg)


def _mixer_a_body(tm, h_ref, g_ref, win_ref, dw_ref, db_ref, lng_ref, lnb_ref,
                  kt_ref, v_ref, qg_ref, seg_ref, wo_ref, o_ref, buf_ref, prim_ref):
    @pl.when(pl.program_id(1) == 0)
    def _():
        buf_ref[0, 0:HALO, :] = jnp.zeros((HALO, CONV_CH), F32)

    h = h_ref[...]
    hn = _rms(h, g_ref[...]).astype(BF16)
    u = jnp.dot(hn, win_ref[...], preferred_element_type=F32)
    a = u[:, :CONV_CH]
    gate = u[:, CONV_CH:2 * CONV_CH]
    buf_ref[0, HALO:HALO + tm, :] = a * jax.nn.sigmoid(gate)

    x_all = buf_ref[0]
    for r in range(1, SUBLANES):
        buf_ref[r] = pltpu.roll(x_all, r, 0)

    def conv_chunk(c, carry):
        base = pl.multiple_of(c * CONV_ROWS, CONV_ROWS)
        acc = jnp.zeros((CONV_ROWS, CONV_CH), F32)
        for j in range(CONV_WIDTH):
            k = CONV_WIDTH - 1 - j
            start = base + (HALO - SUBLANES * (j // SUBLANES))
            acc = acc + dw_ref[k:k + 1, :] * buf_ref[j % SUBLANES, pl.ds(start, CONV_ROWS), :]
        c = acc + db_ref[...]
        mu = jnp.mean(c, axis=-1, keepdims=True)
        xc = c - mu
        var = jnp.mean(xc * xc, axis=-1, keepdims=True)
        y = xc * lax.rsqrt(var + EPS) * lng_ref[...] + lnb_ref[...]
        prim_ref[pl.ds(base, CONV_ROWS), :] = (y * jax.nn.sigmoid(y)).astype(BF16)
        return carry

    lax.fori_loop(0, tm // CONV_ROWS, conv_chunk, 0)
    buf_ref[0, 0:HALO, :] = buf_ref[0, tm:tm + HALO, :]

    mem = _mem_attention(u[:, 2 * CONV_CH:], kt_ref[...], v_ref[...], qg_ref[...], seg_ref[...])
    o_ref[...] = (h
                  + jnp.dot(prim_ref[...], wo_ref[0:CONV_CH, :], preferred_element_type=F32)
                  + jnp.dot(mem.astype(BF16), wo_ref[CONV_CH:, :], preferred_element_type=F32))


def _mixer_a(h, g, win, dw, db, lng, lnb, kt, v, qg, seg, wo, *, tm=512):
    b, s, d = h.shape
    m = kt.shape[2]
    const = lambda i, j: (0, 0)
    return pl.pallas_call(
        functools.partial(_mixer_a_body, tm),
        grid=(b, s // tm),
        in_specs=[
            pl.BlockSpec((None, tm, d), lambda i, j: (i, j, 0)),
            pl.BlockSpec((1, d), const),
            pl.BlockSpec(win.shape, const),
            pl.BlockSpec(dw.shape, const),
            pl.BlockSpec((1, CONV_CH), const),
            pl.BlockSpec((1, CONV_CH), const),
            pl.BlockSpec((1, CONV_CH), const),
            pl.BlockSpec((None, MEM_WIDTH, m), lambda i, j: (i, 0, 0)),
            pl.BlockSpec((None, m, MEM_WIDTH), lambda i, j: (i, 0, 0)),
            pl.BlockSpec((1, MEM_WIDTH), const),
            pl.BlockSpec((MEM_WIDTH, MEM_WIDTH), const),
            pl.BlockSpec((d, d), const),
        ],
        out_specs=pl.BlockSpec((None, tm, d), lambda i, j: (i, j, 0)),
        out_shape=jax.ShapeDtypeStruct((b, s, d), F32),
        scratch_shapes=[pltpu.VMEM((SUBLANES, tm + HALO, CONV_CH), F32),
                        pltpu.VMEM((tm, CONV_CH), BF16)],
        compiler_params=_params(("arbitrary", "arbitrary"), 48),
        name="mixer_a",
    )(h, g, win, dw, db, lng, lnb, kt, v, qg, seg, wo)


def _rope_tab_body(pos_ref, invf_ref, cos_ref, sin_ref):
    tm = pos_ref.shape[1]
    ang = invf_ref[...] * pos_ref[...].astype(F32)
    c = jnp.cos(ang)
    s = jnp.sin(ang)
    row = lax.broadcasted_iota(jnp.int32, ang.shape, 0)
    s = jnp.where(row < ROPE_DIM // 2, -s, s)
    rest = HEAD_DIM - ROPE_DIM
    ones = jnp.ones((rest, tm), F32)
    zeros = jnp.zeros((rest, tm), F32)
    cos_ref[...] = jnp.concatenate([c, ones, c, ones], axis=0).T
    sin_ref[...] = jnp.concatenate([s, zeros, s, zeros], axis=0).T


def _rope_tables(pos3, invf, *, tm=512):
    b, _, s = pos3.shape
    return pl.pallas_call(
        _rope_tab_body,
        grid=(b, s // tm),
        in_specs=[
            pl.BlockSpec((None, 1, tm), lambda i, j: (i, 0, j)),
            pl.BlockSpec((ROPE_DIM, 1), lambda i, j: (0, 0)),
        ],
        out_specs=[pl.BlockSpec((None, tm, LANES), lambda i, j: (i, j, 0)),
                   pl.BlockSpec((None, tm, LANES), lambda i, j: (i, j, 0))],
        out_shape=[jax.ShapeDtypeStruct((b, s, LANES), F32),
                   jax.ShapeDtypeStruct((b, s, LANES), F32)],
        compiler_params=_params(("arbitrary", "arbitrary"), 32),
        name="rope_tab",
    )(pos3, invf)


def _shared_kv_body(tm, h_ref, g_ref, w_ref, kg_ref, seg_ref, cos_ref, sin_ref,
                    k_ref, vt_ref, km_ref):
    hn = _rms(h_ref[...], g_ref[...]).astype(BF16)
    kv = jnp.dot(hn, w_ref[...], preferred_element_type=F32)
    kn = _head_rms(kv[:, :PRIMARY_WIDTH], kg_ref[...], seg_ref[...])
    kr = _rope(kn, cos_ref[...], sin_ref[...])
    k_ref[...] = kr.astype(BF16)
    v = kv[:, PRIMARY_WIDTH:]
    for blk in range(tm // MOBA_BLOCK):
        rows = slice(blk * MOBA_BLOCK, (blk + 1) * MOBA_BLOCK)
        km_ref[blk] = jnp.mean(kr[rows], axis=0, keepdims=True)
        vt_ref[blk] = v[rows].T.astype(BF16)


def _shared_kv(h, g, w, kg, seg, cos, sin, *, tm=512):
    b, s, d = h.shape
    nb = s // MOBA_BLOCK
    bpt = tm // MOBA_BLOCK
    const = lambda i, j: (0, 0)
    return pl.pallas_call(
        functools.partial(_shared_kv_body, tm),
        grid=(b, s // tm),
        in_specs=[
            pl.BlockSpec((None, tm, d), lambda i, j: (i, j, 0)),
            pl.BlockSpec((1, d), const),
            pl.BlockSpec(w.shape, const),
            pl.BlockSpec((1, PRIMARY_WIDTH), const),
            pl.BlockSpec((PRIMARY_WIDTH, PRIMARY_WIDTH), const),
            pl.BlockSpec((None, tm, LANES), lambda i, j: (i, j, 0)),
            pl.BlockSpec((None, tm, LANES), lambda i, j: (i, j, 0)),
        ],
        out_specs=[
            pl.BlockSpec((None, tm, PRIMARY_WIDTH), lambda i, j: (i, j, 0)),
            pl.BlockSpec((None, bpt, PRIMARY_WIDTH, MOBA_BLOCK), lambda i, j: (i, j, 0, 0)),
            pl.BlockSpec((None, bpt, 1, PRIMARY_WIDTH), lambda i, j: (i, j, 0, 0)),
        ],
        out_shape=[jax.ShapeDtypeStruct((b, s, PRIMARY_WIDTH), BF16),
                   jax.ShapeDtypeStruct((b, nb, PRIMARY_WIDTH, MOBA_BLOCK), BF16),
                   jax.ShapeDtypeStruct((b, nb, 1, PRIMARY_WIDTH), F32)],
        compiler_params=_params(("arbitrary", "arbitrary"), 48),
        name="shared_kv",
    )(h, g, w, kg, seg, cos, sin)


def _proj_b_body(h_ref, g_ref, win_ref, qg_ref, seg_ref, cos_ref, sin_ref,
                 kt_ref, v_ref, mqg_ref, mseg_ref, qt_ref, mem_ref):
    hn = _rms(h_ref[...], g_ref[...]).astype(BF16)
    u = jnp.dot(hn, win_ref[...], preferred_element_type=F32)
    qn = _head_rms(u[:, :PRIMARY_WIDTH], qg_ref[...], seg_ref[...])
    qr = _rope(qn, cos_ref[...], sin_ref[...]) * SCALE
    qt_ref[...] = qr.T.astype(BF16)
    mem = _mem_attention(u[:, PRIMARY_WIDTH:], kt_ref[...], v_ref[...], mqg_ref[...], mseg_ref[...])
    mem_ref[...] = mem.astype(BF16)


def _proj_b(h, g, win, qg, seg, cos, sin, kt, v, mqg, mseg, *, tm=512):
    b, s, d = h.shape
    m = kt.shape[2]
    const = lambda i, j: (0, 0)
    return pl.pallas_call(
        _proj_b_body,
        grid=(b, s // tm),
        in_specs=[
            pl.BlockSpec((None, tm, d), lambda i, j: (i, j, 0)),
            pl.BlockSpec((1, d), const),
            pl.BlockSpec(win.shape, const),
            pl.BlockSpec((1, PRIMARY_WIDTH), const),
            pl.BlockSpec((PRIMARY_WIDTH, PRIMARY_WIDTH), const),
            pl.BlockSpec((None, tm, LANES), lambda i, j: (i, j, 0)),
            pl.BlockSpec((None, tm, LANES), lambda i, j: (i, j, 0)),
            pl.BlockSpec((None, MEM_WIDTH, m), lambda i, j: (i, 0, 0)),
            pl.BlockSpec((None, m, MEM_WIDTH), lambda i, j: (i, 0, 0)),
            pl.BlockSpec((1, MEM_WIDTH), const),
            pl.BlockSpec((MEM_WIDTH, MEM_WIDTH), const),
        ],
        out_specs=[pl.BlockSpec((None, PRIMARY_WIDTH, tm), lambda i, j: (i, 0, j)),
                   pl.BlockSpec((None, tm, MEM_WIDTH), lambda i, j: (i, j, 0))],
        out_shape=[jax.ShapeDtypeStruct((b, PRIMARY_WIDTH, s), BF16),
                   jax.ShapeDtypeStruct((b, s, MEM_WIDTH), BF16)],
        compiler_params=_params(("arbitrary", "arbitrary"), 48),
        name="proj_b",
    )(h, g, win, qg, seg, cos, sin, kt, v, mqg, mseg)


def _moba_body(qt_ref, k_ref, vt_ref, km_ref, o_ref, bias_ref, acc_ref):
    qi = pl.program_id(1)
    nb = km_ref.shape[0] // B_HEADS
    tq = qt_ref.shape[1]
    qt = qt_ref[...]

    gate = jnp.dot(km_ref[...], qt, preferred_element_type=F32)
    blk = lax.broadcasted_iota(jnp.int32, (nb, tq), 0)
    for h in range(B_HEADS):
        g = gate[h * nb:(h + 1) * nb, :]
        rank = jnp.zeros((nb, tq), F32)
        for m in range(nb):
            gm = g[m:m + 1, :]
            beats = jnp.where(gm > g, 1.0, jnp.where((gm == g) & (m < blk), 1.0, 0.0))
            rank = rank + jnp.where(m < qi, beats, 0.0)
        bias_ref[h * nb:(h + 1) * nb, :] = jnp.where(rank < MOBA_TOPK, 0.0, MASKED)

    key_pos = lax.broadcasted_iota(jnp.int32, (MOBA_BLOCK, tq), 0)
    q_pos = lax.broadcasted_iota(jnp.int32, (MOBA_BLOCK, tq), 1)
    causal = key_pos <= q_pos
    pair_row = lax.broadcasted_iota(jnp.int32, (LANES, tq), 0)

    for h in range(B_HEADS):
        pair = h // 2
        lanes = pl.ds(pair * LANES, LANES)
        rows = pl.ds(h * HEAD_DIM, HEAD_DIM)
        in_head = (pair_row // HEAD_DIM) == (h % 2)
        qh = jnp.where(in_head, qt[pair * LANES:(pair + 1) * LANES, :], jnp.zeros((LANES, tq), BF16))

        s = jnp.dot(k_ref[qi, :, lanes], qh, preferred_element_type=F32)
        s = jnp.where(causal, s, MASKED)
        m0 = jnp.max(s, axis=0, keepdims=True)
        p = jnp.exp(s - m0)
        l0 = jnp.sum(p, axis=0, keepdims=True)
        acc0 = jnp.dot(vt_ref[qi, rows, :], p.astype(BF16), preferred_element_type=F32)

        def past_block(kj, carry, h=h, lanes=lanes, rows=rows, qh=qh):
            m, l, acc = carry
            s = jnp.dot(k_ref[kj, :, lanes], qh, preferred_element_type=F32)
            s = s + bias_ref[pl.ds(h * nb + kj, 1), :]
            mn = jnp.maximum(m, jnp.max(s, axis=0, keepdims=True))
            alpha = jnp.exp(m - mn)
            p = jnp.exp(s - mn)
            l = alpha * l + jnp.sum(p, axis=0, keepdims=True)
            acc = alpha * acc + jnp.dot(vt_ref[kj, rows, :], p.astype(BF16),
                                        preferred_element_type=F32)
            return mn, l, acc

        _, l, acc = lax.fori_loop(0, qi, past_block, (m0, l0, acc0))
        acc_ref[h * HEAD_DIM:(h + 1) * HEAD_DIM, :] = acc / l

    o_ref[...] = acc_ref[...].T.astype(BF16)


def _moba(qt, kblk, vtblk, kmbd):
    b, w, s = qt.shape
    nb = s // MOBA_BLOCK
    return pl.pallas_call(
        _moba_body,
        grid=(b, nb),
        in_specs=[
            pl.BlockSpec((None, w, MOBA_BLOCK), lambda i, j: (i, 0, j)),
            pl.BlockSpec((None, nb, MOBA_BLOCK, w), lambda i, j: (i, 0, 0, 0)),
            pl.BlockSpec((None, nb, w, MOBA_BLOCK), lambda i, j: (i, 0, 0, 0)),
            pl.BlockSpec((None, B_HEADS * nb, w), lambda i, j: (i, 0, 0)),
        ],
        out_specs=pl.BlockSpec((None, MOBA_BLOCK, w), lambda i, j: (i, j, 0)),
        out_shape=jax.ShapeDtypeStruct((b, s, w), BF16),
        scratch_shapes=[pltpu.VMEM((B_HEADS * nb, MOBA_BLOCK), F32),
                        pltpu.VMEM((w, MOBA_BLOCK), F32)],
        compiler_params=_params(("arbitrary", "arbitrary"), 48),
        name="moba",
    )(qt, kblk, vtblk, kmbd)


def _out_b_body(h_ref, prim_ref, mem_ref, wo_ref, o_ref):
    o_ref[...] = (h_ref[...]
                  + jnp.dot(prim_ref[...], wo_ref[0:PRIMARY_WIDTH, :], preferred_element_type=F32)
                  + jnp.dot(mem_ref[...], wo_ref[PRIMARY_WIDTH:, :], preferred_element_type=F32))


def _out_b(h2d, prim2d, mem2d, wo, *, tm=1024):
    t, d = h2d.shape
    return pl.pallas_call(
        _out_b_body,
        grid=(t // tm,),
        in_specs=[
            pl.BlockSpec((tm, d), lambda i: (i, 0)),
            pl.BlockSpec((tm, PRIMARY_WIDTH), lambda i: (i, 0)),
            pl.BlockSpec((tm, MEM_WIDTH), lambda i: (i, 0)),
            pl.BlockSpec((d, d), lambda i: (0, 0)),
        ],
        out_specs=pl.BlockSpec((tm, d), lambda i: (i, 0)),
        out_shape=jax.ShapeDtypeStruct((t, d), F32),
        compiler_params=_params(("arbitrary",), 48),
        name="out_b",
    )(h2d, prim2d, mem2d, wo)


def _row(v):
    return v.reshape(1, -1).astype(F32)


def _tiled_row(v, reps):
    return jnp.tile(v.astype(F32), reps).reshape(1, -1)


def kernel(x, mem, positions, ffn1_norm_g, ffn1_w_gate, ffn1_w_up, ffn1_w_down, mix_norm_g, mem_norm_g, w_mem_kv, mem_q_norm_g, mem_k_norm_g, w_o, ffn2_norm_g, ffn2_w_gate, ffn2_w_up, ffn2_w_down, a_w_in, a_dw_kernel, a_dw_bias, a_ln_g, a_ln_b, kv_norm_g, w_kv, k_norm_g, b_w_in, b_q_norm_g):
    b, s, d = x.shape
    t = b * s
    nb = s // MOBA_BLOCK
    seg_mem = _seg_mean_matrix(MEM_WIDTH)
    seg_primary = _seg_mean_matrix(PRIMARY_WIDTH)

    def ffn(h, norm_g, wg, wu, wd, layer):
        out = _ffn(h.reshape(t, d), _row(norm_g[layer]), wg[layer].astype(BF16),
                   wu[layer].astype(BF16), wd[layer].astype(BF16))
        return out.reshape(b, s, d)

    def memkv(layer):
        return _memkv(mem, _row(mem_norm_g[layer]), w_mem_kv[layer].astype(BF16),
                      _tiled_row(mem_k_norm_g[layer], MEM_HEADS), seg_mem)

    h = ffn(x, ffn1_norm_g, ffn1_w_gate, ffn1_w_up, ffn1_w_down, 0)
    kt0, v0 = memkv(0)
    dw = jnp.pad(a_dw_kernel[0].reshape(CONV_WIDTH, CONV_CH), ((0, 1), (0, 0)))
    h = _mixer_a(h, _row(mix_norm_g[0]), a_w_in[0].astype(BF16), dw, _row(a_dw_bias[0]),
                 _row(a_ln_g[0]), _row(a_ln_b[0]), kt0, v0,
                 _tiled_row(mem_q_norm_g[0], MEM_HEADS), seg_mem, w_o[0].astype(BF16))
    h = ffn(h, ffn2_norm_g, ffn2_w_gate, ffn2_w_up, ffn2_w_down, 0)

    inv_freq = 1.0 / (ROPE_THETA ** (jnp.arange(0, ROPE_DIM, 2, dtype=F32) / ROPE_DIM))
    invf = jnp.concatenate([inv_freq, inv_freq]).reshape(ROPE_DIM, 1)
    cos, sin = _rope_tables(positions.reshape(b, 1, s), invf)
    k, vt, km = _shared_kv(h, _row(kv_norm_g), w_kv.astype(BF16),
                           _tiled_row(k_norm_g, B_HEADS), seg_primary, cos, sin)
    kblk = k.reshape(b, nb, MOBA_BLOCK, PRIMARY_WIDTH)
    kmh = km.reshape(b, nb, B_HEADS, HEAD_DIM).transpose(0, 2, 1, 3)
    eye = jnp.eye(B_HEADS, dtype=F32)
    kmbd = (kmh[:, :, :, None, :] * eye[None, :, None, :, None]).reshape(
        b, B_HEADS * nb, PRIMARY_WIDTH).astype(BF16)

    h = ffn(h, ffn1_norm_g, ffn1_w_gate, ffn1_w_up, ffn1_w_down, 1)
    kt1, v1 = memkv(1)
    qt, mem_out = _proj_b(h, _row(mix_norm_g[1]), b_w_in[0].astype(BF16),
                          _tiled_row(b_q_norm_g[0], B_HEADS), seg_primary, cos, sin, kt1, v1,
                          _tiled_row(mem_q_norm_g[1], MEM_HEADS), seg_mem)
    prim = _moba(qt, kblk, vt, kmbd)
    h = _out_b(h.reshape(t, d), prim.reshape(t, PRIMARY_WIDTH), mem_out.reshape(t, MEM_WIDTH),
               w_o[1].astype(BF16)).reshape(b, s, d)
    h = ffn(h, ffn2_norm_g, ffn2_w_gate, ffn2_w_up, ffn2_w_down, 1)
    return h
```

```python
import functools

import numpy as np
import jax
import jax.numpy as jnp
from jax import lax
from jax.experimental import pallas as pl
from jax.experimental.pallas import tpu as pltpu

F32 = jnp.float32
BF16 = jnp.bfloat16

D_MODEL = 1024
HEAD_DIM = 64
MEM_HEADS = 4
MEM_WIDTH = MEM_HEADS * HEAD_DIM
PRIMARY_WIDTH = D_MODEL - MEM_WIDTH
B_HEADS = PRIMARY_WIDTH // HEAD_DIM
CONV_CH = PRIMARY_WIDTH
CONV_WIDTH = 31
MOBA_BLOCK = 256
MOBA_TOPK = 3
ROPE_THETA = 500000.0
ROPE_DIM = HEAD_DIM // 4
EPS = 1e-6
SCALE = HEAD_DIM ** -0.5
LOG2E = float(np.log2(np.e))

LANES = 128
SUBLANES = 8
HALO = 32
CONV_ROWS = 32
FFN_CHUNK = 256
MASKED = -1e30
MIB = 1024 * 1024


def _params(semantics, vmem_mib):
    return pltpu.CompilerParams(dimension_semantics=semantics,
                                vmem_limit_bytes=vmem_mib * MIB)


def _rms(x, g):
    ms = jnp.mean(x * x, axis=-1, keepdims=True)
    return x * lax.rsqrt(ms + EPS) * g


def _head_rms(x, g, seg_mean):
    ms = jnp.dot((x * x).astype(BF16), seg_mean, preferred_element_type=F32)
    return x * lax.rsqrt(ms + EPS) * g


def _seg_mean_matrix(width):
    idx = np.arange(width) // HEAD_DIM
    return jnp.asarray((idx[:, None] == idx[None, :]).astype(np.float32) / HEAD_DIM, dtype=BF16)


def _rope(x, cos, sin):
    lane = lax.broadcasted_iota(jnp.int32, (1, LANES), 1) % HEAD_DIM
    first_half = lane < (ROPE_DIM // 2)
    outs = []
    for c in range(x.shape[1] // LANES):
        xc = x[:, c * LANES:(c + 1) * LANES]
        partner = jnp.where(first_half,
                            pltpu.roll(xc, LANES - ROPE_DIM // 2, 1),
                            pltpu.roll(xc, ROPE_DIM // 2, 1))
        outs.append(xc * cos + partner * sin)
    return jnp.concatenate(outs, axis=1)


def _mem_attention(qm, kt, v, qg, seg_mean):
    qn = _head_rms(qm, qg, seg_mean) * SCALE
    lane_head = lax.broadcasted_iota(jnp.int32, (1, MEM_WIDTH), 1) // HEAD_DIM
    out = jnp.zeros(qm.shape, F32)
    for h in range(MEM_HEADS):
        qh = jnp.where(lane_head == h, qn, 0.0).astype(BF16)
        s = jnp.dot(qh, kt, preferred_element_type=F32)
        m = jnp.max(s, axis=-1, keepdims=True)
        p = jnp.exp(s - m)
        l = jnp.sum(p, axis=-1, keepdims=True)
        vh = jnp.where(lane_head == h, v, jnp.zeros_like(v))
        out = out + jnp.dot(p.astype(BF16), vh, preferred_element_type=F32) / l
    return out


def _ffn_body(x_ref, g_ref, wg_ref, wu_ref, wd_ref, o_ref, hmid_ref):
    x = x_ref[...]
    xn = _rms(x, g_ref[...]).astype(BF16)
    for c in range(wg_ref.shape[1] // FFN_CHUNK):
        cols = pl.ds(c * FFN_CHUNK, FFN_CHUNK)
        gate = jnp.dot(xn, wg_ref[:, cols], preferred_element_type=F32)
        up = jnp.dot(xn, wu_ref[:, cols], preferred_element_type=F32)
        hmid_ref[:, cols] = (gate * jax.nn.sigmoid(gate) * up).astype(BF16)
    o_ref[...] = x + 0.5 * jnp.dot(hmid_ref[...], wd_ref[...], preferred_element_type=F32)


def _ffn(h2d, g, wg, wu, wd, *, tm=512):
    t, d = h2d.shape
    f = wg.shape[1]
    resident = dict(pipeline_mode=pl.Buffered(1))
    return pl.pallas_call(
        _ffn_body,
        grid=(t // tm,),
        in_specs=[
            pl.BlockSpec((tm, d), lambda i: (i, 0)),
            pl.BlockSpec((1, d), lambda i: (0, 0)),
            pl.BlockSpec((d, f), lambda i: (0, 0), **resident),
            pl.BlockSpec((d, f), lambda i: (0, 0), **resident),
            pl.BlockSpec((f, d), lambda i: (0, 0), **resident),
        ],
        out_specs=pl.BlockSpec((tm, d), lambda i: (i, 0)),
        out_shape=jax.ShapeDtypeStruct((t, d), F32),
        scratch_shapes=[pltpu.VMEM((tm, f), BF16)],
        compiler_params=_params(("arbitrary",), 48),
        name="ffn",
    )(h2d, g, wg, wu, wd)


def _memkv_body(mem_ref, g_ref, w_ref, kg_ref, seg_ref, kt_ref, v_ref):
    mn = _rms(mem_ref[...], g_ref[...]).astype(BF16)
    kv = jnp.dot(mn, w_ref[...], preferred_element_type=F32)
    k = _head_rms(kv[:, :MEM_WIDTH], kg_ref[...], seg_ref[...])
    kt_ref[...] = k.T.astype(BF16)
    v_ref[...] = kv[:, MEM_WIDTH:].astype(BF16)


def _memkv(mem, g, w, kg, seg):
    b, m, d = mem.shape
    return pl.pallas_call(
        _memkv_body,
        grid=(b,),
        in_specs=[
            pl.BlockSpec((None, m, d), lambda i: (i, 0, 0)),
            pl.BlockSpec((1, d), lambda i: (0, 0)),
            pl.BlockSpec((d, 2 * MEM_WIDTH), lambda i: (0, 0)),
            pl.BlockSpec((1, MEM_WIDTH), lambda i: (0, 0)),
            pl.BlockSpec((MEM_WIDTH, MEM_WIDTH), lambda i: (0, 0)),
        ],
        out_specs=[
            pl.BlockSpec((None, MEM_WIDTH, m), lambda i: (i, 0, 0)),
            pl.BlockSpec((None, m, MEM_WIDTH), lambda i: (i, 0, 0)),
        ],
        out_shape=[jax.ShapeDtypeStruct((b, MEM_WIDTH, m), BF16),
                   jax.ShapeDtypeStruct((b, m, MEM_WIDTH), BF16)],
        compiler_params=_params(("arbitrary",), 32),
        name="memkv",
    )(mem, g, w, kg, seg)


def _mixer_a_body(tm, h_ref, g_ref, win_ref, dw_ref, db_ref, lng_ref, lnb_ref,
                  kt_ref, v_ref, qg_ref, seg_ref, wo_ref, o_ref, buf_ref, prim_ref):
    @pl.when(pl.program_id(1) == 0)
    def _():
        buf_ref[0, 0:HALO, :] = jnp.zeros((HALO, CONV_CH), F32)

    h = h_ref[...]
    hn = _rms(h, g_ref[...]).astype(BF16)
    u = jnp.dot(hn, win_ref[...], preferred_element_type=F32)
    a = u[:, :CONV_CH]
    gate = u[:, CONV_CH:2 * CONV_CH]
    buf_ref[0, HALO:HALO + tm, :] = a * jax.nn.sigmoid(gate)

    x_all = buf_ref[0]
    for r in range(1, SUBLANES):
        buf_ref[r] = pltpu.roll(x_all, r, 0)

    def conv_chunk(c, carry):
        base = pl.multiple_of(c * CONV_ROWS, CONV_ROWS)
        acc = jnp.zeros((CONV_ROWS, CONV_CH), F32)
        for j in range(CONV_WIDTH):
            k = CONV_WIDTH - 1 - j
            start = base + (HALO - SUBLANES * (j // SUBLANES))
            w8 = dw_ref[k * SUBLANES:(k + 1) * SUBLANES, :]
            w = jnp.concatenate([w8] * (CONV_ROWS // SUBLANES), axis=0)
            acc = acc + w * buf_ref[j % SUBLANES, pl.ds(start, CONV_ROWS), :]
        c = acc + db_ref[...]
        mu = jnp.mean(c, axis=-1, keepdims=True)
        xc = c - mu
        var = jnp.mean(xc * xc, axis=-1, keepdims=True)
        y = xc * lax.rsqrt(var + EPS) * lng_ref[...] + lnb_ref[...]
        prim_ref[pl.ds(base, CONV_ROWS), :] = (y * jax.nn.sigmoid(y)).astype(BF16)
        return carry

    lax.fori_loop(0, tm // CONV_ROWS, conv_chunk, 0)
    buf_ref[0, 0:HALO, :] = buf_ref[0, tm:tm + HALO, :]

    mem = _mem_attention(u[:, 2 * CONV_CH:], kt_ref[...], v_ref[...], qg_ref[...], seg_ref[...])
    o_ref[...] = (h
                  + jnp.dot(prim_ref[...], wo_ref[0:CONV_CH, :], preferred_element_type=F32)
                  + jnp.dot(mem.astype(BF16), wo_ref[CONV_CH:, :], preferred_element_type=F32))


def _mixer_a(h, g, win, dw, db, lng, lnb, kt, v, qg, seg, wo, *, tm=512):
    b, s, d = h.shape
    m = kt.shape[2]
    const = lambda i, j: (0, 0)
    return pl.pallas_call(
        functools.partial(_mixer_a_body, tm),
        grid=(b, s // tm),
        in_specs=[
            pl.BlockSpec((None, tm, d), lambda i, j: (i, j, 0)),
            pl.BlockSpec((1, d), const),
            pl.BlockSpec(win.shape, const),
            pl.BlockSpec(dw.shape, const),
            pl.BlockSpec((1, CONV_CH), const),
            pl.BlockSpec((1, CONV_CH), const),
            pl.BlockSpec((1, CONV_CH), const),
            pl.BlockSpec((None, MEM_WIDTH, m), lambda i, j: (i, 0, 0)),
            pl.BlockSpec((None, m, MEM_WIDTH), lambda i, j: (i, 0, 0)),
            pl.BlockSpec((1, MEM_WIDTH), const),
            pl.BlockSpec((MEM_WIDTH, MEM_WIDTH), const),
            pl.BlockSpec((d, d), const),
        ],
        out_specs=pl.BlockSpec((None, tm, d), lambda i, j: (i, j, 0)),
        out_shape=jax.ShapeDtypeStruct((b, s, d), F32),
        scratch_shapes=[pltpu.VMEM((SUBLANES, tm + HALO, CONV_CH), F32),
                        pltpu.VMEM((tm, CONV_CH), BF16)],
        compiler_params=_params(("arbitrary", "arbitrary"), 48),
        name="mixer_a",
    )(h, g, win, dw, db, lng, lnb, kt, v, qg, seg, wo)


def _rope_tab_body(pos_ref, invf_ref, cos_ref, sin_ref):
    tm = pos_ref.shape[1]
    ang = invf_ref[...] * pos_ref[...].astype(F32)
    c = jnp.cos(ang)
    s = jnp.sin(ang)
    row = lax.broadcasted_iota(jnp.int32, ang.shape, 0)
    s = jnp.where(row < ROPE_DIM // 2, -s, s)
    rest = HEAD_DIM - ROPE_DIM
    ones = jnp.ones((rest, tm), F32)
    zeros = jnp.zeros((rest, tm), F32)
    cos_ref[...] = jnp.concatenate([c, ones, c, ones], axis=0).T
    sin_ref[...] = jnp.concatenate([s, zeros, s, zeros], axis=0).T


def _rope_tables(pos3, invf, *, tm=512):
    b, _, s = pos3.shape
    return pl.pallas_call(
        _rope_tab_body,
        grid=(b, s // tm),
        in_specs=[
            pl.BlockSpec((None, 1, tm), lambda i, j: (i, 0, j)),
            pl.BlockSpec((ROPE_DIM, 1), lambda i, j: (0, 0)),
        ],
        out_specs=[pl.BlockSpec((None, tm, LANES), lambda i, j: (i, j, 0)),
                   pl.BlockSpec((None, tm, LANES), lambda i, j: (i, j, 0))],
        out_shape=[jax.ShapeDtypeStruct((b, s, LANES), F32),
                   jax.ShapeDtypeStruct((b, s, LANES), F32)],
        compiler_params=_params(("arbitrary", "arbitrary"), 32),
        name="rope_tab",
    )(pos3, invf)


def _shared_kv_body(tm, h_ref, g_ref, w_ref, kg_ref, seg_ref, cos_ref, sin_ref,
                    k_ref, vt_ref, km_ref):
    hn = _rms(h_ref[...], g_ref[...]).astype(BF16)
    kv = jnp.dot(hn, w_ref[...], preferred_element_type=F32)
    kn = _head_rms(kv[:, :PRIMARY_WIDTH], kg_ref[...], seg_ref[...])
    kr = _rope(kn, cos_ref[...], sin_ref[...])
    k_ref[...] = kr.astype(BF16)
    v = kv[:, PRIMARY_WIDTH:]
    for blk in range(tm // MOBA_BLOCK):
        rows = slice(blk * MOBA_BLOCK, (blk + 1) * MOBA_BLOCK)
        km_ref[blk] = jnp.mean(kr[rows], axis=0, keepdims=True)
        vt_ref[blk] = v[rows].T.astype(BF16)


def _shared_kv(h, g, w, kg, seg, cos, sin, *, tm=512):
    b, s, d = h.shape
    nb = s // MOBA_BLOCK
    bpt = tm // MOBA_BLOCK
    const = lambda i, j: (0, 0)
    return pl.pallas_call(
        functools.partial(_shared_kv_body, tm),
        grid=(b, s // tm),
        in_specs=[
            pl.BlockSpec((None, tm, d), lambda i, j: (i, j, 0)),
            pl.BlockSpec((1, d), const),
            pl.BlockSpec(w.shape, const),
            pl.BlockSpec((1, PRIMARY_WIDTH), const),
            pl.BlockSpec((PRIMARY_WIDTH, PRIMARY_WIDTH), const),
            pl.BlockSpec((None, tm, LANES), lambda i, j: (i, j, 0)),
            pl.BlockSpec((None, tm, LANES), lambda i, j: (i, j, 0)),
        ],
        out_specs=[
            pl.BlockSpec((None, tm, PRIMARY_WIDTH), lambda i, j: (i, j, 0)),
            pl.BlockSpec((None, bpt, PRIMARY_WIDTH, MOBA_BLOCK), lambda i, j: (i, j, 0, 0)),
            pl.BlockSpec((None, bpt, 1, PRIMARY_WIDTH), lambda i, j: (i, j, 0, 0)),
        ],
        out_shape=[jax.ShapeDtypeStruct((b, s, PRIMARY_WIDTH), BF16),
                   jax.ShapeDtypeStruct((b, nb, PRIMARY_WIDTH, MOBA_BLOCK), BF16),
                   jax.ShapeDtypeStruct((b, nb, 1, PRIMARY_WIDTH), F32)],
        compiler_params=_params(("arbitrary", "arbitrary"), 48),
        name="shared_kv",
    )(h, g, w, kg, seg, cos, sin)


def _proj_b_body(h_ref, g_ref, win_ref, qg_ref, seg_ref, cos_ref, sin_ref,
                 kt_ref, v_ref, mqg_ref, mseg_ref, qt_ref, mem_ref):
    hn = _rms(h_ref[...], g_ref[...]).astype(BF16)
    u = jnp.dot(hn, win_ref[...], preferred_element_type=F32)
    qn = _head_rms(u[:, :PRIMARY_WIDTH], qg_ref[...], seg_ref[...])
    qr = _rope(qn, cos_ref[...], sin_ref[...]) * (SCALE * LOG2E)
    qt_ref[...] = qr.T.astype(BF16)
    mem = _mem_attention(u[:, PRIMARY_WIDTH:], kt_ref[...], v_ref[...], mqg_ref[...], mseg_ref[...])
    mem_ref[...] = mem.astype(BF16)


def _proj_b(h, g, win, qg, seg, cos, sin, kt, v, mqg, mseg, *, tm=512):
    b, s, d = h.shape
    m = kt.shape[2]
    const = lambda i, j: (0, 0)
    return pl.pallas_call(
        _proj_b_body,
        grid=(b, s // tm),
        in_specs=[
            pl.BlockSpec((None, tm, d), lambda i, j: (i, j, 0)),
            pl.BlockSpec((1, d), const),
            pl.BlockSpec(win.shape, const),
            pl.BlockSpec((1, PRIMARY_WIDTH), const),
            pl.BlockSpec((PRIMARY_WIDTH, PRIMARY_WIDTH), const),
            pl.BlockSpec((None, tm, LANES), lambda i, j: (i, j, 0)),
            pl.BlockSpec((None, tm, LANES), lambda i, j: (i, j, 0)),
            pl.BlockSpec((None, MEM_WIDTH, m), lambda i, j: (i, 0, 0)),
            pl.BlockSpec((None, m, MEM_WIDTH), lambda i, j: (i, 0, 0)),
            pl.BlockSpec((1, MEM_WIDTH), const),
            pl.BlockSpec((MEM_WIDTH, MEM_WIDTH), const),
        ],
        out_specs=[pl.BlockSpec((None, PRIMARY_WIDTH, tm), lambda i, j: (i, 0, j)),
                   pl.BlockSpec((None, tm, MEM_WIDTH), lambda i, j: (i, j, 0))],
        out_shape=[jax.ShapeDtypeStruct((b, PRIMARY_WIDTH, s), BF16),
                   jax.ShapeDtypeStruct((b, s, MEM_WIDTH), BF16)],
        compiler_params=_params(("arbitrary", "arbitrary"), 48),
        name="proj_b",
    )(h, g, win, qg, seg, cos, sin, kt, v, mqg, mseg)


def _moba_body(qt_ref, k_ref, vt_ref, km_ref, o_ref,
               bias_ref, qh_ref, s_ref, p_ref, m_ref, l_ref, shift_ref, alpha_ref, acc_ref):
    qi = pl.program_id(1)
    nb = km_ref.shape[0] // B_HEADS
    tq = qt_ref.shape[1]
    qt = qt_ref[...]

    gate = jnp.dot(km_ref[...], qt, preferred_element_type=F32)
    blk = lax.broadcasted_iota(jnp.int32, (nb, tq), 0)
    for h in range(B_HEADS):
        g = gate[h * nb:(h + 1) * nb, :]
        rank = jnp.zeros((nb, tq), F32)
        for m in range(nb):
            gm = g[m:m + 1, :]
            beats = jnp.where(gm > g, 1.0, jnp.where((gm == g) & (m < blk), 1.0, 0.0))
            rank = rank + jnp.where(m < qi, beats, 0.0)
        bias_ref[h * nb:(h + 1) * nb, :] = jnp.where(rank < MOBA_TOPK, 0.0, MASKED)

    pair_row = lax.broadcasted_iota(jnp.int32, (LANES, tq), 0)
    for h in range(B_HEADS):
        pair = h // 2
        in_head = (pair_row // HEAD_DIM) == (h % 2)
        qh_ref[h] = jnp.where(in_head, qt[pair * LANES:(pair + 1) * LANES, :],
                              jnp.zeros((LANES, tq), BF16))

    key_pos = lax.broadcasted_iota(jnp.int32, (MOBA_BLOCK, tq), 0)
    q_pos = lax.broadcasted_iota(jnp.int32, (MOBA_BLOCK, tq), 1)
    causal = key_pos <= q_pos

    def attend(kj, own):
        for h in range(B_HEADS):
            lanes = pl.ds((h // 2) * LANES, LANES)
            s = jnp.dot(k_ref[kj, :, lanes], qh_ref[h], preferred_element_type=F32)
            if own:
                s = jnp.where(causal, s, MASKED)
            s_ref[h] = s
        for h in range(B_HEADS):
            blk_max = jnp.max(s_ref[h], axis=0, keepdims=True)
            if own:
                m_ref[h:h + 1, :] = blk_max
                shift_ref[h:h + 1, :] = blk_max
            else:
                bias = bias_ref[pl.ds(h * nb + kj, 1), :]
                m_old = m_ref[h:h + 1, :]
                m_new = jnp.maximum(m_old, blk_max + bias)
                shift_ref[h:h + 1, :] = m_new - bias
                alpha_ref[h:h + 1, :] = jnp.exp2(m_old - m_new)
                m_ref[h:h + 1, :] = m_new
        for h in range(B_HEADS):
            p = jnp.exp2(s_ref[h] - shift_ref[h:h + 1, :])
            psum = jnp.sum(p, axis=0, keepdims=True)
            p_ref[h] = p.astype(BF16)
            if own:
                l_ref[h:h + 1, :] = psum
            else:
                l_ref[h:h + 1, :] = alpha_ref[h:h + 1, :] * l_ref[h:h + 1, :] + psum
        for h in range(B_HEADS):
            rows = pl.ds(h * HEAD_DIM, HEAD_DIM)
            pv = jnp.dot(vt_ref[kj, rows, :], p_ref[h], preferred_element_type=F32)
            if own:
                acc_ref[rows, :] = pv
            else:
                acc_ref[rows, :] = alpha_ref[h:h + 1, :] * acc_ref[rows, :] + pv

    attend(qi, True)

    def past_block(kj, carry):
        attend(kj, False)
        return carry

    lax.fori_loop(0, qi, past_block, 0)

    for h in range(B_HEADS):
        rows = pl.ds(h * HEAD_DIM, HEAD_DIM)
        acc_ref[rows, :] = acc_ref[rows, :] / l_ref[h:h + 1, :]
    o_ref[...] = acc_ref[...].T.astype(BF16)


def _moba(qt, kblk, vtblk, kmbd):
    b, w, s = qt.shape
    nb = s // MOBA_BLOCK
    return pl.pallas_call(
        _moba_body,
        grid=(b, nb),
        in_specs=[
            pl.BlockSpec((None, w, MOBA_BLOCK), lambda i, j: (i, 0, j)),
            pl.BlockSpec((None, nb, MOBA_BLOCK, w), lambda i, j: (i, 0, 0, 0)),
            pl.BlockSpec((None, nb, w, MOBA_BLOCK), lambda i, j: (i, 0, 0, 0)),
            pl.BlockSpec((None, B_HEADS * nb, w), lambda i, j: (i, 0, 0)),
        ],
        out_specs=pl.BlockSpec((None, MOBA_BLOCK, w), lambda i, j: (i, j, 0)),
        out_shape=jax.ShapeDtypeStruct((b, s, w), BF16),
        scratch_shapes=[pltpu.VMEM((B_HEADS * nb, MOBA_BLOCK), F32),
                        pltpu.VMEM((B_HEADS, LANES, MOBA_BLOCK), BF16),
                        pltpu.VMEM((B_HEADS, MOBA_BLOCK, MOBA_BLOCK), F32),
                        pltpu.VMEM((B_HEADS, MOBA_BLOCK, MOBA_BLOCK), BF16),
                        pltpu.VMEM((2 * SUBLANES, MOBA_BLOCK), F32),
                        pltpu.VMEM((2 * SUBLANES, MOBA_BLOCK), F32),
                        pltpu.VMEM((2 * SUBLANES, MOBA_BLOCK), F32),
                        pltpu.VMEM((2 * SUBLANES, MOBA_BLOCK), F32),
                        pltpu.VMEM((w, MOBA_BLOCK), F32)],
        compiler_params=_params(("arbitrary", "arbitrary"), 48),
        name="moba",
    )(qt, kblk, vtblk, kmbd)


def _out_b_body(h_ref, prim_ref, mem_ref, wo_ref, o_ref):
    o_ref[...] = (h_ref[...]
                  + jnp.dot(prim_ref[...], wo_ref[0:PRIMARY_WIDTH, :], preferred_element_type=F32)
                  + jnp.dot(mem_ref[...], wo_ref[PRIMARY_WIDTH:, :], preferred_element_type=F32))


def _out_b(h2d, prim2d, mem2d, wo, *, tm=1024):
    t, d = h2d.shape
    return pl.pallas_call(
        _out_b_body,
        grid=(t // tm,),
        in_specs=[
            pl.BlockSpec((tm, d), lambda i: (i, 0)),
            pl.BlockSpec((tm, PRIMARY_WIDTH), lambda i: (i, 0)),
            pl.BlockSpec((tm, MEM_WIDTH), lambda i: (i, 0)),
            pl.BlockSpec((d, d), lambda i: (0, 0)),
        ],
        out_specs=pl.BlockSpec((tm, d), lambda i: (i, 0)),
        out_shape=jax.ShapeDtypeStruct((t, d), F32),
        compiler_params=_params(("arbitrary",), 48),
        name="out_b",
    )(h2d, prim2d, mem2d, wo)


def _row(v):
    return v.reshape(1, -1).astype(F32)


def _tiled_row(v, reps):
    return jnp.tile(v.astype(F32), reps).reshape(1, -1)


def kernel(x, mem, positions, ffn1_norm_g, ffn1_w_gate, ffn1_w_up, ffn1_w_down, mix_norm_g, mem_norm_g, w_mem_kv, mem_q_norm_g, mem_k_norm_g, w_o, ffn2_norm_g, ffn2_w_gate, ffn2_w_up, ffn2_w_down, a_w_in, a_dw_kernel, a_dw_bias, a_ln_g, a_ln_b, kv_norm_g, w_kv, k_norm_g, b_w_in, b_q_norm_g):
    b, s, d = x.shape
    t = b * s
    nb = s // MOBA_BLOCK
    seg_mem = _seg_mean_matrix(MEM_WIDTH)
    seg_primary = _seg_mean_matrix(PRIMARY_WIDTH)

    def ffn(h, norm_g, wg, wu, wd, layer):
        out = _ffn(h.reshape(t, d), _row(norm_g[layer]), wg[layer].astype(BF16),
                   wu[layer].astype(BF16), wd[layer].astype(BF16))
        return out.reshape(b, s, d)

    def memkv(layer):
        return _memkv(mem, _row(mem_norm_g[layer]), w_mem_kv[layer].astype(BF16),
                      _tiled_row(mem_k_norm_g[layer], MEM_HEADS), seg_mem)

    h = ffn(x, ffn1_norm_g, ffn1_w_gate, ffn1_w_up, ffn1_w_down, 0)
    kt0, v0 = memkv(0)
    dw = jnp.repeat(a_dw_kernel[0].reshape(CONV_WIDTH, CONV_CH), SUBLANES, axis=0)
    h = _mixer_a(h, _row(mix_norm_g[0]), a_w_in[0].astype(BF16), dw, _row(a_dw_bias[0]),
                 _row(a_ln_g[0]), _row(a_ln_b[0]), kt0, v0,
                 _tiled_row(mem_q_norm_g[0], MEM_HEADS), seg_mem, w_o[0].astype(BF16))
    h = ffn(h, ffn2_norm_g, ffn2_w_gate, ffn2_w_up, ffn2_w_down, 0)

    inv_freq = 1.0 / (ROPE_THETA ** (jnp.arange(0, ROPE_DIM, 2, dtype=F32) / ROPE_DIM))
    invf = jnp.concatenate([inv_freq, inv_freq]).reshape(ROPE_DIM, 1)
    cos, sin = _rope_tables(positions.reshape(b, 1, s), invf)
    k, vt, km = _shared_kv(h, _row(kv_norm_g), w_kv.astype(BF16),
                           _tiled_row(k_norm_g, B_HEADS), seg_primary, cos, sin)
    kblk = k.reshape(b, nb, MOBA_BLOCK, PRIMARY_WIDTH)
    kmh = km.reshape(b, nb, B_HEADS, HEAD_DIM).transpose(0, 2, 1, 3)
    eye = jnp.eye(B_HEADS, dtype=F32)
    kmbd = (kmh[:, :, :, None, :] * eye[None, :, None, :, None]).reshape(
        b, B_HEADS * nb, PRIMARY_WIDTH).astype(BF16)

    h = ffn(h, ffn1_norm_g, ffn1_w_gate, ffn1_w_up, ffn1_w_down, 1)
    kt1, v1 = memkv(1)
    qt, mem_out = _proj_b(h, _row(mix_norm_g[1]), b_w_in[0].astype(BF16),
                          _tiled_row(b_q_norm_g[0], B_HEADS), seg_primary, cos, sin, kt1, v1,
                          _tiled_row(mem_q_norm_g[1], MEM_HEADS), seg_mem)
    prim = _moba(qt, kblk, vt, kmbd)
    h = _out_b(h.reshape(t, d), prim.reshape(t, PRIMARY_WIDTH), mem_out.reshape(t, MEM_WIDTH),
               w_o[1].astype(BF16)).reshape(b, s, d)
    h = ffn(h, ffn2_norm_g, ffn2_w_gate, ffn2_w_up, ffn2_w_down, 1)
    return h
```

```python
import functools

import numpy as np
import jax
import jax.numpy as jnp
from jax import lax
from jax.experimental import pallas as pl
from jax.experimental.pallas import tpu as pltpu

F32 = jnp.float32
BF16 = jnp.bfloat16

D_MODEL = 1024
HEAD_DIM = 64
MEM_HEADS = 4
MEM_WIDTH = MEM_HEADS * HEAD_DIM
PRIMARY_WIDTH = D_MODEL - MEM_WIDTH
B_HEADS = PRIMARY_WIDTH // HEAD_DIM
CONV_CH = PRIMARY_WIDTH
CONV_WIDTH = 31
MOBA_BLOCK = 256
MOBA_TOPK = 3
ROPE_THETA = 500000.0
ROPE_DIM = HEAD_DIM // 4
EPS = 1e-6
SCALE = HEAD_DIM ** -0.5
LOG2E = float(np.log2(np.e))

LANES = 128
SUBLANES = 8
HALO = 32
CONV_ROWS = 64
FFN_CHUNK = 256
MASKED = -1e30
VT_ROWS = HEAD_DIM + 16
SCORE_BOUND_PER_GAIN = 1.05 * HEAD_DIM * SCALE * LOG2E
MAX_FIXED_SHIFT = 60.0
MIB = 1024 * 1024


def _params(semantics, vmem_mib):
    return pltpu.CompilerParams(dimension_semantics=semantics,
                                vmem_limit_bytes=vmem_mib * MIB)


def _rms(x, g):
    ms = jnp.mean(x * x, axis=-1, keepdims=True)
    return x * lax.rsqrt(ms + EPS) * g


def _head_rms(x, g, seg_mean):
    ms = jnp.dot((x * x).astype(BF16), seg_mean, preferred_element_type=F32)
    return x * lax.rsqrt(ms + EPS) * g


def _seg_mean_matrix(width):
    idx = np.arange(width) // HEAD_DIM
    return jnp.asarray((idx[:, None] == idx[None, :]).astype(np.float32) / HEAD_DIM, dtype=BF16)


def _rope(x, cos, sin):
    lane = lax.broadcasted_iota(jnp.int32, (1, LANES), 1) % HEAD_DIM
    first_half = lane < (ROPE_DIM // 2)
    outs = []
    for c in range(x.shape[1] // LANES):
        xc = x[:, c * LANES:(c + 1) * LANES]
        partner = jnp.where(first_half,
                            pltpu.roll(xc, LANES - ROPE_DIM // 2, 1),
                            pltpu.roll(xc, ROPE_DIM // 2, 1))
        outs.append(xc * cos + partner * sin)
    return jnp.concatenate(outs, axis=1)


def _mem_attention(qm, kt, v, qg, seg_mean):
    qn = _head_rms(qm, qg, seg_mean) * SCALE
    lane_head = lax.broadcasted_iota(jnp.int32, (1, MEM_WIDTH), 1) // HEAD_DIM
    out = jnp.zeros(qm.shape, F32)
    for h in range(MEM_HEADS):
        qh = jnp.where(lane_head == h, qn, 0.0).astype(BF16)
        s = jnp.dot(qh, kt, preferred_element_type=F32)
        m = jnp.max(s, axis=-1, keepdims=True)
        p = jnp.exp(s - m)
        l = jnp.sum(p, axis=-1, keepdims=True)
        vh = jnp.where(lane_head == h, v, jnp.zeros_like(v))
        out = out + jnp.dot(p.astype(BF16), vh, preferred_element_type=F32) / l
    return out


def _ffn_body(has_mix, *refs):
    if has_mix:
        prim_ref, mem_ref, wo_ref, *refs = refs
    x_ref, g_ref, wg_ref, wu_ref, wd_ref, o_ref, hmid_ref = refs
    x = x_ref[...]
    if has_mix:
        x = (x + jnp.dot(prim_ref[...], wo_ref[0:PRIMARY_WIDTH, :], preferred_element_type=F32)
             + jnp.dot(mem_ref[...], wo_ref[PRIMARY_WIDTH:, :], preferred_element_type=F32))
    xn = _rms(x, g_ref[...]).astype(BF16)
    for c in range(wg_ref.shape[1] // FFN_CHUNK):
        cols = pl.ds(c * FFN_CHUNK, FFN_CHUNK)
        gate = jnp.dot(xn, wg_ref[:, cols], preferred_element_type=F32)
        up = jnp.dot(xn, wu_ref[:, cols], preferred_element_type=F32)
        hmid_ref[:, cols] = (gate * jax.nn.sigmoid(gate) * up).astype(BF16)
    o_ref[...] = x + 0.5 * jnp.dot(hmid_ref[...], wd_ref[...], preferred_element_type=F32)


def _ffn(h2d, g, wg, wu, wd, mix=None, *, tm=512):
    t, d = h2d.shape
    f = wg.shape[1]
    resident = dict(pipeline_mode=pl.Buffered(1))
    mix_args, mix_specs = (), []
    if mix is not None:
        mix_args = mix
        mix_specs = [
            pl.BlockSpec((tm, PRIMARY_WIDTH), lambda i: (i, 0)),
            pl.BlockSpec((tm, MEM_WIDTH), lambda i: (i, 0)),
            pl.BlockSpec((d, d), lambda i: (0, 0), **resident),
        ]
    return pl.pallas_call(
        functools.partial(_ffn_body, mix is not None),
        grid=(t // tm,),
        in_specs=mix_specs + [
            pl.BlockSpec((tm, d), lambda i: (i, 0)),
            pl.BlockSpec((1, d), lambda i: (0, 0)),
            pl.BlockSpec((d, f), lambda i: (0, 0), **resident),
            pl.BlockSpec((d, f), lambda i: (0, 0), **resident),
            pl.BlockSpec((f, d), lambda i: (0, 0), **resident),
        ],
        out_specs=pl.BlockSpec((tm, d), lambda i: (i, 0)),
        out_shape=jax.ShapeDtypeStruct((t, d), F32),
        scratch_shapes=[pltpu.VMEM((tm, f), BF16)],
        compiler_params=_params(("arbitrary",), 48),
        name="ffn_mix" if mix is not None else "ffn",
    )(*mix_args, h2d, g, wg, wu, wd)


def _memkv_body(mem_ref, g_ref, w_ref, kg_ref, seg_ref, kt_ref, v_ref):
    mn = _rms(mem_ref[...], g_ref[...]).astype(BF16)
    kv = jnp.dot(mn, w_ref[...], preferred_element_type=F32)
    k = _head_rms(kv[:, :MEM_WIDTH], kg_ref[...], seg_ref[...])
    kt_ref[...] = k.T.astype(BF16)
    v_ref[...] = kv[:, MEM_WIDTH:].astype(BF16)


def _memkv(mem, g, w, kg, seg):
    b, m, d = mem.shape
    return pl.pallas_call(
        _memkv_body,
        grid=(b,),
        in_specs=[
            pl.BlockSpec((None, m, d), lambda i: (i, 0, 0)),
            pl.BlockSpec((1, d), lambda i: (0, 0)),
            pl.BlockSpec((d, 2 * MEM_WIDTH), lambda i: (0, 0)),
            pl.BlockSpec((1, MEM_WIDTH), lambda i: (0, 0)),
            pl.BlockSpec((MEM_WIDTH, MEM_WIDTH), lambda i: (0, 0)),
        ],
        out_specs=[
            pl.BlockSpec((None, MEM_WIDTH, m), lambda i: (i, 0, 0)),
            pl.BlockSpec((None, m, MEM_WIDTH), lambda i: (i, 0, 0)),
        ],
        out_shape=[jax.ShapeDtypeStruct((b, MEM_WIDTH, m), BF16),
                   jax.ShapeDtypeStruct((b, m, MEM_WIDTH), BF16)],
        compiler_params=_params(("arbitrary",), 32),
        name="memkv",
    )(mem, g, w, kg, seg)


def _mixer_a_body(tm, h_ref, g_ref, win_ref, dw_ref, db_ref, lng_ref, lnb_ref,
                  kt_ref, v_ref, qg_ref, seg_ref, wo_ref, o_ref, buf_ref, conv_ref):
    @pl.when(pl.program_id(1) == 0)
    def _():
        buf_ref[0, 0:HALO, :] = jnp.zeros((HALO, CONV_CH), F32)

    h = h_ref[...]
    hn = _rms(h, g_ref[...]).astype(BF16)
    u = jnp.dot(hn, win_ref[...], preferred_element_type=F32)
    a = u[:, :CONV_CH]
    gate = u[:, CONV_CH:2 * CONV_CH]
    buf_ref[0, HALO:HALO + tm, :] = a * jax.nn.sigmoid(gate)

    x_all = buf_ref[0]
    for r in range(1, SUBLANES):
        buf_ref[r] = pltpu.roll(x_all, r, 0)

    groups = CONV_ROWS // SUBLANES
    for cb in range(CONV_CH // LANES):
        lanes = pl.ds(cb * LANES, LANES)

        def conv_chunk(c, carry, lanes=lanes):
            base = pl.multiple_of(c * CONV_ROWS, CONV_ROWS)
            accs = [jnp.zeros((SUBLANES, LANES), F32)] * groups
            for j in range(CONV_WIDTH):
                k = CONV_WIDTH - 1 - j
                start = base + (HALO - SUBLANES * (j // SUBLANES))
                w8 = dw_ref[k * SUBLANES:(k + 1) * SUBLANES, lanes]
                xs = buf_ref[j % SUBLANES, pl.ds(start, CONV_ROWS), lanes]
                accs = [accs[i] + w8 * xs[i * SUBLANES:(i + 1) * SUBLANES] for i in range(groups)]
            conv_ref[pl.ds(base, CONV_ROWS), lanes] = jnp.concatenate(accs, axis=0)
            return carry

        lax.fori_loop(0, tm // CONV_ROWS, conv_chunk, 0, unroll=2)
    buf_ref[0, 0:HALO, :] = buf_ref[0, tm:tm + HALO, :]

    c = conv_ref[...] + db_ref[...]
    mu = jnp.mean(c, axis=-1, keepdims=True)
    xc = c - mu
    var = jnp.mean(xc * xc, axis=-1, keepdims=True)
    y = xc * lax.rsqrt(var + EPS) * lng_ref[...] + lnb_ref[...]
    prim = (y * jax.nn.sigmoid(y)).astype(BF16)

    mem = _mem_attention(u[:, 2 * CONV_CH:], kt_ref[...], v_ref[...], qg_ref[...], seg_ref[...])
    o_ref[...] = (h
                  + jnp.dot(prim, wo_ref[0:CONV_CH, :], preferred_element_type=F32)
                  + jnp.dot(mem.astype(BF16), wo_ref[CONV_CH:, :], preferred_element_type=F32))


def _mixer_a(h, g, win, dw, db, lng, lnb, kt, v, qg, seg, wo, *, tm=512):
    b, s, d = h.shape
    m = kt.shape[2]
    const = lambda i, j: (0, 0)
    return pl.pallas_call(
        functools.partial(_mixer_a_body, tm),
        grid=(b, s // tm),
        in_specs=[
            pl.BlockSpec((None, tm, d), lambda i, j: (i, j, 0)),
            pl.BlockSpec((1, d), const),
            pl.BlockSpec(win.shape, const),
            pl.BlockSpec(dw.shape, const),
            pl.BlockSpec((1, CONV_CH), const),
            pl.BlockSpec((1, CONV_CH), const),
            pl.BlockSpec((1, CONV_CH), const),
            pl.BlockSpec((None, MEM_WIDTH, m), lambda i, j: (i, 0, 0)),
            pl.BlockSpec((None, m, MEM_WIDTH), lambda i, j: (i, 0, 0)),
            pl.BlockSpec((1, MEM_WIDTH), const),
            pl.BlockSpec((MEM_WIDTH, MEM_WIDTH), const),
            pl.BlockSpec((d, d), const),
        ],
        out_specs=pl.BlockSpec((None, tm, d), lambda i, j: (i, j, 0)),
        out_shape=jax.ShapeDtypeStruct((b, s, d), F32),
        scratch_shapes=[pltpu.VMEM((SUBLANES, tm + HALO, CONV_CH), F32),
                        pltpu.VMEM((tm, CONV_CH), F32)],
        compiler_params=_params(("arbitrary", "arbitrary"), 48),
        name="mixer_a",
    )(h, g, win, dw, db, lng, lnb, kt, v, qg, seg, wo)


def _rope_tab_body(pos_ref, invf_ref, cos_ref, sin_ref):
    tm = pos_ref.shape[1]
    ang = invf_ref[...] * pos_ref[...].astype(F32)
    c = jnp.cos(ang)
    s = jnp.sin(ang)
    row = lax.broadcasted_iota(jnp.int32, ang.shape, 0)
    s = jnp.where(row < ROPE_DIM // 2, -s, s)
    rest = HEAD_DIM - ROPE_DIM
    ones = jnp.ones((rest, tm), F32)
    zeros = jnp.zeros((rest, tm), F32)
    cos_ref[...] = jnp.concatenate([c, ones, c, ones], axis=0).T
    sin_ref[...] = jnp.concatenate([s, zeros, s, zeros], axis=0).T


def _rope_tables(pos3, invf, *, tm=512):
    b, _, s = pos3.shape
    return pl.pallas_call(
        _rope_tab_body,
        grid=(b, s // tm),
        in_specs=[
            pl.BlockSpec((None, 1, tm), lambda i, j: (i, 0, j)),
            pl.BlockSpec((ROPE_DIM, 1), lambda i, j: (0, 0)),
        ],
        out_specs=[pl.BlockSpec((None, tm, LANES), lambda i, j: (i, j, 0)),
                   pl.BlockSpec((None, tm, LANES), lambda i, j: (i, j, 0))],
        out_shape=[jax.ShapeDtypeStruct((b, s, LANES), F32),
                   jax.ShapeDtypeStruct((b, s, LANES), F32)],
        compiler_params=_params(("arbitrary", "arbitrary"), 32),
        name="rope_tab",
    )(pos3, invf)


def _shared_kv_body(tm, h_ref, g_ref, w_ref, kg_ref, seg_ref, cos_ref, sin_ref,
                    k_ref, vt_ref, km_ref):
    hn = _rms(h_ref[...], g_ref[...]).astype(BF16)
    kv = jnp.dot(hn, w_ref[...], preferred_element_type=F32)
    kn = _head_rms(kv[:, :PRIMARY_WIDTH], kg_ref[...], seg_ref[...])
    kr = _rope(kn, cos_ref[...], sin_ref[...])
    k_ref[...] = kr.astype(BF16)
    v = kv[:, PRIMARY_WIDTH:]
    ones = jnp.ones((VT_ROWS - HEAD_DIM, MOBA_BLOCK), BF16)
    for blk in range(tm // MOBA_BLOCK):
        rows = slice(blk * MOBA_BLOCK, (blk + 1) * MOBA_BLOCK)
        km_ref[blk] = jnp.mean(kr[rows], axis=0, keepdims=True)
        vt = v[rows].T.astype(BF16)
        for h in range(B_HEADS):
            vt_ref[blk, h * VT_ROWS:h * VT_ROWS + HEAD_DIM, :] = vt[h * HEAD_DIM:(h + 1) * HEAD_DIM]
            vt_ref[blk, h * VT_ROWS + HEAD_DIM:(h + 1) * VT_ROWS, :] = ones


def _shared_kv(h, g, w, kg, seg, cos, sin, *, tm=512):
    b, s, d = h.shape
    nb = s // MOBA_BLOCK
    bpt = tm // MOBA_BLOCK
    const = lambda i, j: (0, 0)
    return pl.pallas_call(
        functools.partial(_shared_kv_body, tm),
        grid=(b, s // tm),
        in_specs=[
            pl.BlockSpec((None, tm, d), lambda i, j: (i, j, 0)),
            pl.BlockSpec((1, d), const),
            pl.BlockSpec(w.shape, const),
            pl.BlockSpec((1, PRIMARY_WIDTH), const),
            pl.BlockSpec((PRIMARY_WIDTH, PRIMARY_WIDTH), const),
            pl.BlockSpec((None, tm, LANES), lambda i, j: (i, j, 0)),
            pl.BlockSpec((None, tm, LANES), lambda i, j: (i, j, 0)),
        ],
        out_specs=[
            pl.BlockSpec((None, tm, PRIMARY_WIDTH), lambda i, j: (i, j, 0)),
            pl.BlockSpec((None, bpt, B_HEADS * VT_ROWS, MOBA_BLOCK), lambda i, j: (i, j, 0, 0)),
            pl.BlockSpec((None, bpt, 1, PRIMARY_WIDTH), lambda i, j: (i, j, 0, 0)),
        ],
        out_shape=[jax.ShapeDtypeStruct((b, s, PRIMARY_WIDTH), BF16),
                   jax.ShapeDtypeStruct((b, nb, B_HEADS * VT_ROWS, MOBA_BLOCK), BF16),
                   jax.ShapeDtypeStruct((b, nb, 1, PRIMARY_WIDTH), F32)],
        compiler_params=_params(("arbitrary", "arbitrary"), 48),
        name="shared_kv",
    )(h, g, w, kg, seg, cos, sin)


def _proj_b_body(h_ref, g_ref, win_ref, qg_ref, seg_ref, cos_ref, sin_ref,
                 kt_ref, v_ref, mqg_ref, mseg_ref, qt_ref, mem_ref):
    hn = _rms(h_ref[...], g_ref[...]).astype(BF16)
    u = jnp.dot(hn, win_ref[...], preferred_element_type=F32)
    qn = _head_rms(u[:, :PRIMARY_WIDTH], qg_ref[...], seg_ref[...])
    qr = _rope(qn, cos_ref[...], sin_ref[...]) * (SCALE * LOG2E)
    qt_ref[...] = qr.T.astype(BF16)
    mem = _mem_attention(u[:, PRIMARY_WIDTH:], kt_ref[...], v_ref[...], mqg_ref[...], mseg_ref[...])
    mem_ref[...] = mem.astype(BF16)


def _proj_b(h, g, win, qg, seg, cos, sin, kt, v, mqg, mseg, *, tm=512):
    b, s, d = h.shape
    m = kt.shape[2]
    const = lambda i, j: (0, 0)
    return pl.pallas_call(
        _proj_b_body,
        grid=(b, s // tm),
        in_specs=[
            pl.BlockSpec((None, tm, d), lambda i, j: (i, j, 0)),
            pl.BlockSpec((1, d), const),
            pl.BlockSpec(win.shape, const),
            pl.BlockSpec((1, PRIMARY_WIDTH), const),
            pl.BlockSpec((PRIMARY_WIDTH, PRIMARY_WIDTH), const),
            pl.BlockSpec((None, tm, LANES), lambda i, j: (i, j, 0)),
            pl.BlockSpec((None, tm, LANES), lambda i, j: (i, j, 0)),
            pl.BlockSpec((None, MEM_WIDTH, m), lambda i, j: (i, 0, 0)),
            pl.BlockSpec((None, m, MEM_WIDTH), lambda i, j: (i, 0, 0)),
            pl.BlockSpec((1, MEM_WIDTH), const),
            pl.BlockSpec((MEM_WIDTH, MEM_WIDTH), const),
        ],
        out_specs=[pl.BlockSpec((None, PRIMARY_WIDTH, tm), lambda i, j: (i, 0, j)),
                   pl.BlockSpec((None, tm, MEM_WIDTH), lambda i, j: (i, j, 0))],
        out_shape=[jax.ShapeDtypeStruct((b, PRIMARY_WIDTH, s), BF16),
                   jax.ShapeDtypeStruct((b, s, MEM_WIDTH), BF16)],
        compiler_params=_params(("arbitrary", "arbitrary"), 48),
        name="proj_b",
    )(h, g, win, qg, seg, cos, sin, kt, v, mqg, mseg)


def _moba_body(ctl_ref, qt_ref, k_ref, vt_ref, km_ref, o_ref,
               bias_ref, qh_ref, s_ref, p_ref, m_ref, shift_ref, alpha_ref, acc_ref, out_ref):
    qi = pl.program_id(1)
    nb = km_ref.shape[0] // B_HEADS
    tq = qt_ref.shape[1]
    qt = qt_ref[...]

    gate = jnp.dot(km_ref[...], qt, preferred_element_type=F32)
    blk = lax.broadcasted_iota(jnp.int32, (nb, tq), 0)
    for h in range(B_HEADS):
        g = gate[h * nb:(h + 1) * nb, :]
        rank = jnp.zeros((nb, tq), F32)
        for m in range(nb):
            gm = g[m:m + 1, :]
            beats = jnp.where(gm > g, 1.0, jnp.where((gm == g) & (m < blk), 1.0, 0.0))
            rank = rank + jnp.where(m < qi, beats, 0.0)
        bias_ref[h * nb:(h + 1) * nb, :] = jnp.where(rank < MOBA_TOPK, 0.0, MASKED)

    pair_row = lax.broadcasted_iota(jnp.int32, (LANES, tq), 0)
    for h in range(B_HEADS):
        pair = h // 2
        in_head = (pair_row // HEAD_DIM) == (h % 2)
        qh_ref[h] = jnp.where(in_head, qt[pair * LANES:(pair + 1) * LANES, :],
                              jnp.zeros((LANES, tq), BF16))

    key_pos = lax.broadcasted_iota(jnp.int32, (MOBA_BLOCK, tq), 0)
    q_pos = lax.broadcasted_iota(jnp.int32, (MOBA_BLOCK, tq), 1)
    causal = key_pos <= q_pos

    def scores(kj, h, own):
        lanes = pl.ds((h // 2) * LANES, LANES)
        s = jnp.dot(k_ref[kj, :, lanes], qh_ref[h], preferred_element_type=F32)
        return jnp.where(causal, s, MASKED) if own else s

    def head_rows(h):
        return pl.ds(h * VT_ROWS, VT_ROWS)

    def attend_fixed_shift(kj, own):
        for h in range(B_HEADS):
            shift = bound if own else bound - bias_ref[pl.ds(h * nb + kj, 1), :]
            p_ref[h] = jnp.exp2(scores(kj, h, own) - shift).astype(BF16)
        for h in range(B_HEADS):
            pv = jnp.dot(vt_ref[kj, head_rows(h), :], p_ref[h], preferred_element_type=F32)
            if own:
                acc_ref[head_rows(h), :] = pv
            else:
                acc_ref[head_rows(h), :] += pv

    def attend_running_max(kj, own):
        for h in range(B_HEADS):
            s_ref[h] = scores(kj, h, own)
        for h in range(B_HEADS):
            blk_max = jnp.max(s_ref[h], axis=0, keepdims=True)
            if own:
                m_ref[h:h + 1, :] = blk_max
                shift_ref[h:h + 1, :] = blk_max
            else:
                bias = bias_ref[pl.ds(h * nb + kj, 1), :]
                m_old = m_ref[h:h + 1, :]
                m_new = jnp.maximum(m_old, blk_max + bias)
                shift_ref[h:h + 1, :] = m_new - bias
                alpha_ref[h:h + 1, :] = jnp.exp2(m_old - m_new)
                m_ref[h:h + 1, :] = m_new
        for h in range(B_HEADS):
            p_ref[h] = jnp.exp2(s_ref[h] - shift_ref[h:h + 1, :]).astype(BF16)
        for h in range(B_HEADS):
            pv = jnp.dot(vt_ref[kj, head_rows(h), :], p_ref[h], preferred_element_type=F32)
            if own:
                acc_ref[head_rows(h), :] = pv
            else:
                acc_ref[head_rows(h), :] = alpha_ref[h:h + 1, :] * acc_ref[head_rows(h), :] + pv

    def attend_all(attend):
        attend(qi, True)

        def past_block(kj, carry):
            attend(kj, False)
            return carry

        lax.fori_loop(0, qi, past_block, 0)

    bound = ctl_ref[0]
    use_fixed_shift = ctl_ref[1] > 0.5
    pl.when(use_fixed_shift)(lambda: attend_all(attend_fixed_shift))
    pl.when(jnp.logical_not(use_fixed_shift))(lambda: attend_all(attend_running_max))

    for h in range(B_HEADS):
        num = acc_ref[pl.ds(h * VT_ROWS, HEAD_DIM), :]
        den = acc_ref[pl.ds(h * VT_ROWS + HEAD_DIM, 1), :]
        out_ref[pl.ds(h * HEAD_DIM, HEAD_DIM), :] = num / den
    o_ref[...] = out_ref[...].T.astype(BF16)


def _moba(ctl, qt, kblk, vtblk, kmbd):
    b, w, s = qt.shape
    nb = s // MOBA_BLOCK
    stat = pltpu.VMEM((2 * SUBLANES, MOBA_BLOCK), F32)
    return pl.pallas_call(
        _moba_body,
        grid=(b, nb),
        in_specs=[
            pl.BlockSpec(memory_space=pltpu.SMEM),
            pl.BlockSpec((None, w, MOBA_BLOCK), lambda i, j: (i, 0, j)),
            pl.BlockSpec((None, nb, MOBA_BLOCK, w), lambda i, j: (i, 0, 0, 0)),
            pl.BlockSpec((None, nb, B_HEADS * VT_ROWS, MOBA_BLOCK), lambda i, j: (i, 0, 0, 0)),
            pl.BlockSpec((None, B_HEADS * nb, w), lambda i, j: (i, 0, 0)),
        ],
        out_specs=pl.BlockSpec((None, MOBA_BLOCK, w), lambda i, j: (i, j, 0)),
        out_shape=jax.ShapeDtypeStruct((b, s, w), BF16),
        scratch_shapes=[pltpu.VMEM((B_HEADS * nb, MOBA_BLOCK), F32),
                        pltpu.VMEM((B_HEADS, LANES, MOBA_BLOCK), BF16),
                        pltpu.VMEM((B_HEADS, MOBA_BLOCK, MOBA_BLOCK), F32),
                        pltpu.VMEM((B_HEADS, MOBA_BLOCK, MOBA_BLOCK), BF16),
                        stat, stat, stat,
                        pltpu.VMEM((B_HEADS * VT_ROWS, MOBA_BLOCK), F32),
                        pltpu.VMEM((w, MOBA_BLOCK), F32)],
        compiler_params=_params(("arbitrary", "arbitrary"), 56),
        name="moba",
    )(ctl, qt, kblk, vtblk, kmbd)


def _row(v):
    return v.reshape(1, -1).astype(F32)


def _tiled_row(v, reps):
    return jnp.tile(v.astype(F32), reps).reshape(1, -1)


def kernel(x, mem, positions, ffn1_norm_g, ffn1_w_gate, ffn1_w_up, ffn1_w_down, mix_norm_g, mem_norm_g, w_mem_kv, mem_q_norm_g, mem_k_norm_g, w_o, ffn2_norm_g, ffn2_w_gate, ffn2_w_up, ffn2_w_down, a_w_in, a_dw_kernel, a_dw_bias, a_ln_g, a_ln_b, kv_norm_g, w_kv, k_norm_g, b_w_in, b_q_norm_g):
    b, s, d = x.shape
    t = b * s
    nb = s // MOBA_BLOCK
    seg_mem = _seg_mean_matrix(MEM_WIDTH)
    seg_primary = _seg_mean_matrix(PRIMARY_WIDTH)

    def ffn(h, norm_g, wg, wu, wd, layer, mix=None):
        out = _ffn(h.reshape(t, d), _row(norm_g[layer]), wg[layer].astype(BF16),
                   wu[layer].astype(BF16), wd[layer].astype(BF16), mix)
        return out.reshape(b, s, d)

    def memkv(layer):
        return _memkv(mem, _row(mem_norm_g[layer]), w_mem_kv[layer].astype(BF16),
                      _tiled_row(mem_k_norm_g[layer], MEM_HEADS), seg_mem)

    h = ffn(x, ffn1_norm_g, ffn1_w_gate, ffn1_w_up, ffn1_w_down, 0)
    kt0, v0 = memkv(0)
    dw = jnp.repeat(a_dw_kernel[0].reshape(CONV_WIDTH, CONV_CH), SUBLANES, axis=0)
    h = _mixer_a(h, _row(mix_norm_g[0]), a_w_in[0].astype(BF16), dw, _row(a_dw_bias[0]),
                 _row(a_ln_g[0]), _row(a_ln_b[0]), kt0, v0,
                 _tiled_row(mem_q_norm_g[0], MEM_HEADS), seg_mem, w_o[0].astype(BF16))
    h = ffn(h, ffn2_norm_g, ffn2_w_gate, ffn2_w_up, ffn2_w_down, 0)

    inv_freq = 1.0 / (ROPE_THETA ** (jnp.arange(0, ROPE_DIM, 2, dtype=F32) / ROPE_DIM))
    invf = jnp.concatenate([inv_freq, inv_freq]).reshape(ROPE_DIM, 1)
    cos, sin = _rope_tables(positions.reshape(b, 1, s), invf)
    k, vt, km = _shared_kv(h, _row(kv_norm_g), w_kv.astype(BF16),
                           _tiled_row(k_norm_g, B_HEADS), seg_primary, cos, sin)
    kblk = k.reshape(b, nb, MOBA_BLOCK, PRIMARY_WIDTH)
    kmh = km.reshape(b, nb, B_HEADS, HEAD_DIM).transpose(0, 2, 1, 3)
    eye = jnp.eye(B_HEADS, dtype=F32)
    kmbd = (kmh[:, :, :, None, :] * eye[None, :, None, :, None]).reshape(
        b, B_HEADS * nb, PRIMARY_WIDTH).astype(BF16)

    h = ffn(h, ffn1_norm_g, ffn1_w_gate, ffn1_w_up, ffn1_w_down, 1)
    kt1, v1 = memkv(1)
    qt, mem_out = _proj_b(h, _row(mix_norm_g[1]), b_w_in[0].astype(BF16),
                          _tiled_row(b_q_norm_g[0], B_HEADS), seg_primary, cos, sin, kt1, v1,
                          _tiled_row(mem_q_norm_g[1], MEM_HEADS), seg_mem)
    score_bound = (SCORE_BOUND_PER_GAIN * jnp.max(jnp.abs(b_q_norm_g[0]))
                   * jnp.max(jnp.abs(k_norm_g))).astype(F32)
    ctl = jnp.stack([score_bound, (score_bound <= MAX_FIXED_SHIFT).astype(F32)])
    prim = _moba(ctl, qt, kblk, vt, kmbd)
    mix = (prim.reshape(t, PRIMARY_WIDTH), mem_out.reshape(t, MEM_WIDTH), w_o[1].astype(BF16))
    return ffn(h, ffn2_norm_g, ffn2_w_gate, ffn2_w_up, ffn2_w_down, 1, mix)
```

```python
import functools

import numpy as np
import jax
import jax.numpy as jnp
from jax import lax
from jax.experimental import pallas as pl
from jax.experimental.pallas import tpu as pltpu

F32 = jnp.float32
BF16 = jnp.bfloat16

D_MODEL = 1024
HEAD_DIM = 64
MEM_HEADS = 4
MEM_WIDTH = MEM_HEADS * HEAD_DIM
PRIMARY_WIDTH = D_MODEL - MEM_WIDTH
B_HEADS = PRIMARY_WIDTH // HEAD_DIM
CONV_CH = PRIMARY_WIDTH
CONV_WIDTH = 31
MOBA_BLOCK = 256
MOBA_TOPK = 3
ROPE_THETA = 500000.0
ROPE_DIM = HEAD_DIM // 4
EPS = 1e-6
SCALE = HEAD_DIM ** -0.5
LOG2E = float(np.log2(np.e))

LANES = 128
SUBLANES = 8
HALO = 32
CONV_ROWS = 64
FFN_CHUNK = 256
MASKED = -1e30
VT_ROWS = HEAD_DIM + 16
SCORE_BOUND_PER_GAIN = 1.05 * HEAD_DIM * SCALE * LOG2E
MAX_FIXED_SHIFT = 60.0
MIB = 1024 * 1024


def _params(semantics, vmem_mib):
    return pltpu.CompilerParams(dimension_semantics=semantics,
                                vmem_limit_bytes=vmem_mib * MIB)


def _rms(x, g):
    ms = jnp.mean(x * x, axis=-1, keepdims=True)
    return x * lax.rsqrt(ms + EPS) * g


def _head_rms(x, g, seg_mean):
    ms = jnp.dot((x * x).astype(BF16), seg_mean, preferred_element_type=F32)
    return x * lax.rsqrt(ms + EPS) * g


def _seg_mean_matrix(width):
    idx = np.arange(width) // HEAD_DIM
    return jnp.asarray((idx[:, None] == idx[None, :]).astype(np.float32) / HEAD_DIM, dtype=BF16)


def _rope(x, cos, sin):
    lane = lax.broadcasted_iota(jnp.int32, (1, LANES), 1) % HEAD_DIM
    first_half = lane < (ROPE_DIM // 2)
    outs = []
    for c in range(x.shape[1] // LANES):
        xc = x[:, c * LANES:(c + 1) * LANES]
        partner = jnp.where(first_half,
                            pltpu.roll(xc, LANES - ROPE_DIM // 2, 1),
                            pltpu.roll(xc, ROPE_DIM // 2, 1))
        outs.append(xc * cos + partner * sin)
    return jnp.concatenate(outs, axis=1)


def _mem_attention(qm, kt, v, qg, seg_mean):
    qn = _head_rms(qm, qg, seg_mean) * SCALE
    lane_head = lax.broadcasted_iota(jnp.int32, (1, MEM_WIDTH), 1) // HEAD_DIM
    out = jnp.zeros(qm.shape, F32)
    for h in range(MEM_HEADS):
        qh = jnp.where(lane_head == h, qn, 0.0).astype(BF16)
        s = jnp.dot(qh, kt, preferred_element_type=F32)
        m = jnp.max(s, axis=-1, keepdims=True)
        p = jnp.exp(s - m)
        l = jnp.sum(p, axis=-1, keepdims=True)
        vh = jnp.where(lane_head == h, v, jnp.zeros_like(v))
        out = out + jnp.dot(p.astype(BF16), vh, preferred_element_type=F32) / l
    return out


def _ffn_body(has_mix, *refs):
    if has_mix:
        prim_ref, mem_ref, wo_ref, *refs = refs
    x_ref, g_ref, wg_ref, wu_ref, wd_ref, o_ref, hmid_ref = refs
    x = x_ref[...]
    if has_mix:
        x = (x + jnp.dot(prim_ref[...], wo_ref[0:PRIMARY_WIDTH, :], preferred_element_type=F32)
             + jnp.dot(mem_ref[...], wo_ref[PRIMARY_WIDTH:, :], preferred_element_type=F32))
    xn = _rms(x, g_ref[...]).astype(BF16)
    for c in range(wg_ref.shape[1] // FFN_CHUNK):
        cols = pl.ds(c * FFN_CHUNK, FFN_CHUNK)
        gate = jnp.dot(xn, wg_ref[:, cols], preferred_element_type=F32)
        up = jnp.dot(xn, wu_ref[:, cols], preferred_element_type=F32)
        hmid_ref[:, cols] = (gate * jax.nn.sigmoid(gate) * up).astype(BF16)
    o_ref[...] = x + 0.5 * jnp.dot(hmid_ref[...], wd_ref[...], preferred_element_type=F32)


def _ffn(h2d, g, wg, wu, wd, mix=None, *, tm=512):
    t, d = h2d.shape
    f = wg.shape[1]
    resident = dict(pipeline_mode=pl.Buffered(1))
    mix_args, mix_specs = (), []
    if mix is not None:
        mix_args = mix
        mix_specs = [
            pl.BlockSpec((tm, PRIMARY_WIDTH), lambda i: (i, 0)),
            pl.BlockSpec((tm, MEM_WIDTH), lambda i: (i, 0)),
            pl.BlockSpec((d, d), lambda i: (0, 0), **resident),
        ]
    return pl.pallas_call(
        functools.partial(_ffn_body, mix is not None),
        grid=(t // tm,),
        in_specs=mix_specs + [
            pl.BlockSpec((tm, d), lambda i: (i, 0)),
            pl.BlockSpec((1, d), lambda i: (0, 0)),
            pl.BlockSpec((d, f), lambda i: (0, 0), **resident),
            pl.BlockSpec((d, f), lambda i: (0, 0), **resident),
            pl.BlockSpec((f, d), lambda i: (0, 0), **resident),
        ],
        out_specs=pl.BlockSpec((tm, d), lambda i: (i, 0)),
        out_shape=jax.ShapeDtypeStruct((t, d), F32),
        scratch_shapes=[pltpu.VMEM((tm, f), BF16)],
        compiler_params=_params(("arbitrary",), 56),
        name="ffn_mix" if mix is not None else "ffn",
    )(*mix_args, h2d, g, wg, wu, wd)


def _memkv_body(mem_ref, g_ref, w_ref, kg_ref, seg_ref, kt_ref, v_ref):
    mn = _rms(mem_ref[...], g_ref[...]).astype(BF16)
    kv = jnp.dot(mn, w_ref[...], preferred_element_type=F32)
    k = _head_rms(kv[:, :MEM_WIDTH], kg_ref[...], seg_ref[...])
    kt_ref[...] = k.T.astype(BF16)
    v_ref[...] = kv[:, MEM_WIDTH:].astype(BF16)


def _memkv(mem, g, w, kg, seg):
    b, m, d = mem.shape
    return pl.pallas_call(
        _memkv_body,
        grid=(b,),
        in_specs=[
            pl.BlockSpec((None, m, d), lambda i: (i, 0, 0)),
            pl.BlockSpec((1, d), lambda i: (0, 0)),
            pl.BlockSpec((d, 2 * MEM_WIDTH), lambda i: (0, 0)),
            pl.BlockSpec((1, MEM_WIDTH), lambda i: (0, 0)),
            pl.BlockSpec((MEM_WIDTH, MEM_WIDTH), lambda i: (0, 0)),
        ],
        out_specs=[
            pl.BlockSpec((None, MEM_WIDTH, m), lambda i: (i, 0, 0)),
            pl.BlockSpec((None, m, MEM_WIDTH), lambda i: (i, 0, 0)),
        ],
        out_shape=[jax.ShapeDtypeStruct((b, MEM_WIDTH, m), BF16),
                   jax.ShapeDtypeStruct((b, m, MEM_WIDTH), BF16)],
        compiler_params=_params(("arbitrary",), 32),
        name="memkv",
    )(mem, g, w, kg, seg)


def _mixer_a_body(tm, h_ref, g_ref, win_ref, dw_ref, db_ref, lng_ref, lnb_ref,
                  kt_ref, v_ref, qg_ref, seg_ref, wo_ref, o_ref, buf_ref, conv_ref):
    n_strips = CONV_CH // LANES

    @pl.when(pl.program_id(1) == 0)
    def _():
        buf_ref[0, :, 0:HALO, :] = jnp.zeros((n_strips, HALO, LANES), F32)

    h = h_ref[...]
    hn = _rms(h, g_ref[...]).astype(BF16)
    u = jnp.dot(hn, win_ref[...], preferred_element_type=F32)
    glu = u[:, :CONV_CH] * jax.nn.sigmoid(u[:, CONV_CH:2 * CONV_CH])

    groups = CONV_ROWS // SUBLANES
    for cb in range(n_strips):
        buf_ref[0, cb, HALO:HALO + tm, :] = glu[:, cb * LANES:(cb + 1) * LANES]
        x_strip = buf_ref[0, cb]
        for r in range(1, SUBLANES):
            buf_ref[r, cb] = pltpu.roll(x_strip, r, 0)

        def conv_chunk(c, carry, cb=cb):
            base = pl.multiple_of(c * CONV_ROWS, CONV_ROWS)
            accs = [jnp.zeros((SUBLANES, LANES), F32)] * groups
            for j in range(CONV_WIDTH):
                k = CONV_WIDTH - 1 - j
                start = base + (HALO - SUBLANES * (j // SUBLANES))
                w8 = dw_ref[cb, k * SUBLANES:(k + 1) * SUBLANES, :]
                xs = buf_ref[j % SUBLANES, cb, pl.ds(start, CONV_ROWS), :]
                accs = [accs[i] + w8 * xs[i * SUBLANES:(i + 1) * SUBLANES] for i in range(groups)]
            conv_ref[pl.ds(base, CONV_ROWS), pl.ds(cb * LANES, LANES)] = jnp.concatenate(accs, axis=0)
            return carry

        lax.fori_loop(0, tm // CONV_ROWS, conv_chunk, 0, unroll=2)
    buf_ref[0, :, 0:HALO, :] = buf_ref[0, :, tm:tm + HALO, :]

    c = conv_ref[...] + db_ref[...]
    mu = jnp.mean(c, axis=-1, keepdims=True)
    xc = c - mu
    var = jnp.mean(xc * xc, axis=-1, keepdims=True)
    y = xc * lax.rsqrt(var + EPS) * lng_ref[...] + lnb_ref[...]
    prim = (y * jax.nn.sigmoid(y)).astype(BF16)

    mem = _mem_attention(u[:, 2 * CONV_CH:], kt_ref[...], v_ref[...], qg_ref[...], seg_ref[...])
    o_ref[...] = (h
                  + jnp.dot(prim, wo_ref[0:CONV_CH, :], preferred_element_type=F32)
                  + jnp.dot(mem.astype(BF16), wo_ref[CONV_CH:, :], preferred_element_type=F32))


def _mixer_a(h, g, win, dw, db, lng, lnb, kt, v, qg, seg, wo, *, tm=512):
    b, s, d = h.shape
    m = kt.shape[2]
    const = lambda i, j: (0, 0)
    return pl.pallas_call(
        functools.partial(_mixer_a_body, tm),
        grid=(b, s // tm),
        in_specs=[
            pl.BlockSpec((None, tm, d), lambda i, j: (i, j, 0)),
            pl.BlockSpec((1, d), const),
            pl.BlockSpec(win.shape, const),
            pl.BlockSpec(dw.shape, lambda i, j: (0, 0, 0)),
            pl.BlockSpec((1, CONV_CH), const),
            pl.BlockSpec((1, CONV_CH), const),
            pl.BlockSpec((1, CONV_CH), const),
            pl.BlockSpec((None, MEM_WIDTH, m), lambda i, j: (i, 0, 0)),
            pl.BlockSpec((None, m, MEM_WIDTH), lambda i, j: (i, 0, 0)),
            pl.BlockSpec((1, MEM_WIDTH), const),
            pl.BlockSpec((MEM_WIDTH, MEM_WIDTH), const),
            pl.BlockSpec((d, d), const),
        ],
        out_specs=pl.BlockSpec((None, tm, d), lambda i, j: (i, j, 0)),
        out_shape=jax.ShapeDtypeStruct((b, s, d), F32),
        scratch_shapes=[pltpu.VMEM((SUBLANES, CONV_CH // LANES, tm + HALO, LANES), F32),
                        pltpu.VMEM((tm, CONV_CH), F32)],
        compiler_params=_params(("arbitrary", "arbitrary"), 48),
        name="mixer_a",
    )(h, g, win, dw, db, lng, lnb, kt, v, qg, seg, wo)


def _rope_tab_body(pos_ref, invf_ref, cos_ref, sin_ref):
    tm = pos_ref.shape[1]
    ang = invf_ref[...] * pos_ref[...].astype(F32)
    c = jnp.cos(ang)
    s = jnp.sin(ang)
    row = lax.broadcasted_iota(jnp.int32, ang.shape, 0)
    s = jnp.where(row < ROPE_DIM // 2, -s, s)
    rest = HEAD_DIM - ROPE_DIM
    ones = jnp.ones((rest, tm), F32)
    zeros = jnp.zeros((rest, tm), F32)
    cos_ref[...] = jnp.concatenate([c, ones, c, ones], axis=0).T
    sin_ref[...] = jnp.concatenate([s, zeros, s, zeros], axis=0).T


def _rope_tables(pos3, invf, *, tm=512):
    b, _, s = pos3.shape
    return pl.pallas_call(
        _rope_tab_body,
        grid=(b, s // tm),
        in_specs=[
            pl.BlockSpec((None, 1, tm), lambda i, j: (i, 0, j)),
            pl.BlockSpec((ROPE_DIM, 1), lambda i, j: (0, 0)),
        ],
        out_specs=[pl.BlockSpec((None, tm, LANES), lambda i, j: (i, j, 0)),
                   pl.BlockSpec((None, tm, LANES), lambda i, j: (i, j, 0))],
        out_shape=[jax.ShapeDtypeStruct((b, s, LANES), F32),
                   jax.ShapeDtypeStruct((b, s, LANES), F32)],
        compiler_params=_params(("arbitrary", "arbitrary"), 32),
        name="rope_tab",
    )(pos3, invf)


def _shared_kv_body(tm, h_ref, g_ref, w_ref, kg_ref, seg_ref, cos_ref, sin_ref,
                    k_ref, vt_ref, km_ref):
    hn = _rms(h_ref[...], g_ref[...]).astype(BF16)
    kv = jnp.dot(hn, w_ref[...], preferred_element_type=F32)
    kn = _head_rms(kv[:, :PRIMARY_WIDTH], kg_ref[...], seg_ref[...])
    kr = _rope(kn, cos_ref[...], sin_ref[...])
    k_ref[...] = kr.astype(BF16)
    v = kv[:, PRIMARY_WIDTH:]
    ones = jnp.ones((VT_ROWS - HEAD_DIM, MOBA_BLOCK), BF16)
    for blk in range(tm // MOBA_BLOCK):
        rows = slice(blk * MOBA_BLOCK, (blk + 1) * MOBA_BLOCK)
        km_ref[blk] = jnp.mean(kr[rows], axis=0, keepdims=True)
        vt = v[rows].T.astype(BF16)
        for h in range(B_HEADS):
            vt_ref[blk, h * VT_ROWS:h * VT_ROWS + HEAD_DIM, :] = vt[h * HEAD_DIM:(h + 1) * HEAD_DIM]
            vt_ref[blk, h * VT_ROWS + HEAD_DIM:(h + 1) * VT_ROWS, :] = ones


def _shared_kv(h, g, w, kg, seg, cos, sin, *, tm=512):
    b, s, d = h.shape
    nb = s // MOBA_BLOCK
    bpt = tm // MOBA_BLOCK
    const = lambda i, j: (0, 0)
    return pl.pallas_call(
        functools.partial(_shared_kv_body, tm),
        grid=(b, s // tm),
        in_specs=[
            pl.BlockSpec((None, tm, d), lambda i, j: (i, j, 0)),
            pl.BlockSpec((1, d), const),
            pl.BlockSpec(w.shape, const),
            pl.BlockSpec((1, PRIMARY_WIDTH), const),
            pl.BlockSpec((PRIMARY_WIDTH, PRIMARY_WIDTH), const),
            pl.BlockSpec((None, tm, LANES), lambda i, j: (i, j, 0)),
            pl.BlockSpec((None, tm, LANES), lambda i, j: (i, j, 0)),
        ],
        out_specs=[
            pl.BlockSpec((None, tm, PRIMARY_WIDTH), lambda i, j: (i, j, 0)),
            pl.BlockSpec((None, bpt, B_HEADS * VT_ROWS, MOBA_BLOCK), lambda i, j: (i, j, 0, 0)),
            pl.BlockSpec((None, bpt, 1, PRIMARY_WIDTH), lambda i, j: (i, j, 0, 0)),
        ],
        out_shape=[jax.ShapeDtypeStruct((b, s, PRIMARY_WIDTH), BF16),
                   jax.ShapeDtypeStruct((b, nb, B_HEADS * VT_ROWS, MOBA_BLOCK), BF16),
                   jax.ShapeDtypeStruct((b, nb, 1, PRIMARY_WIDTH), F32)],
        compiler_params=_params(("arbitrary", "arbitrary"), 48),
        name="shared_kv",
    )(h, g, w, kg, seg, cos, sin)


def _proj_b_body(h_ref, g_ref, win_ref, qg_ref, seg_ref, cos_ref, sin_ref,
                 kt_ref, v_ref, mqg_ref, mseg_ref, qt_ref, mem_ref):
    hn = _rms(h_ref[...], g_ref[...]).astype(BF16)
    u = jnp.dot(hn, win_ref[...], preferred_element_type=F32)
    qn = _head_rms(u[:, :PRIMARY_WIDTH], qg_ref[...], seg_ref[...])
    qr = _rope(qn, cos_ref[...], sin_ref[...]) * (SCALE * LOG2E)
    qt_ref[...] = qr.T.astype(BF16)
    mem = _mem_attention(u[:, PRIMARY_WIDTH:], kt_ref[...], v_ref[...], mqg_ref[...], mseg_ref[...])
    mem_ref[...] = mem.astype(BF16)


def _proj_b(h, g, win, qg, seg, cos, sin, kt, v, mqg, mseg, *, tm=512):
    b, s, d = h.shape
    m = kt.shape[2]
    const = lambda i, j: (0, 0)
    return pl.pallas_call(
        _proj_b_body,
        grid=(b, s // tm),
        in_specs=[
            pl.BlockSpec((None, tm, d), lambda i, j: (i, j, 0)),
            pl.BlockSpec((1, d), const),
            pl.BlockSpec(win.shape, const),
            pl.BlockSpec((1, PRIMARY_WIDTH), const),
            pl.BlockSpec((PRIMARY_WIDTH, PRIMARY_WIDTH), const),
            pl.BlockSpec((None, tm, LANES), lambda i, j: (i, j, 0)),
            pl.BlockSpec((None, tm, LANES), lambda i, j: (i, j, 0)),
            pl.BlockSpec((None, MEM_WIDTH, m), lambda i, j: (i, 0, 0)),
            pl.BlockSpec((None, m, MEM_WIDTH), lambda i, j: (i, 0, 0)),
            pl.BlockSpec((1, MEM_WIDTH), const),
            pl.BlockSpec((MEM_WIDTH, MEM_WIDTH), const),
        ],
        out_specs=[pl.BlockSpec((None, PRIMARY_WIDTH, tm), lambda i, j: (i, 0, j)),
                   pl.BlockSpec((None, tm, MEM_WIDTH), lambda i, j: (i, j, 0))],
        out_shape=[jax.ShapeDtypeStruct((b, PRIMARY_WIDTH, s), BF16),
                   jax.ShapeDtypeStruct((b, s, MEM_WIDTH), BF16)],
        compiler_params=_params(("arbitrary", "arbitrary"), 48),
        name="proj_b",
    )(h, g, win, qg, seg, cos, sin, kt, v, mqg, mseg)


def _moba_body(ctl_ref, qt_ref, k_ref, vt_ref, km_ref, o_ref,
               bias_ref, qh_ref, s_ref, p_ref, m_ref, shift_ref, alpha_ref, acc_ref, out_ref):
    qi = pl.program_id(1)
    nb = km_ref.shape[0] // B_HEADS
    tq = qt_ref.shape[1]
    qt = qt_ref[...]

    gate = jnp.dot(km_ref[...], qt, preferred_element_type=F32)
    blk = lax.broadcasted_iota(jnp.int32, (nb, tq), 0)
    for h in range(B_HEADS):
        g = gate[h * nb:(h + 1) * nb, :]
        rank = jnp.zeros((nb, tq), F32)
        for m in range(nb):
            gm = g[m:m + 1, :]
            valid = (m < qi).astype(F32)
            rank = rank + jnp.where(m < blk, jnp.where(gm >= g, valid, 0.0),
                                    jnp.where(gm > g, valid, 0.0))
        bias_ref[h * nb:(h + 1) * nb, :] = jnp.where(rank < MOBA_TOPK, 0.0, MASKED)

    pair_row = lax.broadcasted_iota(jnp.int32, (LANES, tq), 0)
    for h in range(B_HEADS):
        pair = h // 2
        in_head = (pair_row // HEAD_DIM) == (h % 2)
        qh_ref[h] = jnp.where(in_head, qt[pair * LANES:(pair + 1) * LANES, :],
                              jnp.zeros((LANES, tq), BF16))

    key_pos = lax.broadcasted_iota(jnp.int32, (MOBA_BLOCK, tq), 0)
    q_pos = lax.broadcasted_iota(jnp.int32, (MOBA_BLOCK, tq), 1)
    causal = key_pos <= q_pos

    def scores(kj, h, own):
        lanes = pl.ds((h // 2) * LANES, LANES)
        s = jnp.dot(k_ref[kj, :, lanes], qh_ref[h], preferred_element_type=F32)
        return jnp.where(causal, s, MASKED) if own else s

    def head_rows(h):
        return pl.ds(h * VT_ROWS, VT_ROWS)

    def attend_fixed_shift(kj, own):
        for h in range(B_HEADS):
            shift = bound if own else bound - bias_ref[pl.ds(h * nb + kj, 1), :]
            p_ref[h] = jnp.exp2(scores(kj, h, own) - shift).astype(BF16)
        for h in range(B_HEADS):
            pv = jnp.dot(vt_ref[kj, head_rows(h), :], p_ref[h], preferred_element_type=F32)
            if own:
                acc_ref[head_rows(h), :] = pv
            else:
                acc_ref[head_rows(h), :] += pv

    def attend_running_max(kj, own):
        for h in range(B_HEADS):
            s_ref[h] = scores(kj, h, own)
        for h in range(B_HEADS):
            blk_max = jnp.max(s_ref[h], axis=0, keepdims=True)
            if own:
                m_ref[h:h + 1, :] = blk_max
                shift_ref[h:h + 1, :] = blk_max
            else:
                bias = bias_ref[pl.ds(h * nb + kj, 1), :]
                m_old = m_ref[h:h + 1, :]
                m_new = jnp.maximum(m_old, blk_max + bias)
                shift_ref[h:h + 1, :] = m_new - bias
                alpha_ref[h:h + 1, :] = jnp.exp2(m_old - m_new)
                m_ref[h:h + 1, :] = m_new
        for h in range(B_HEADS):
            p_ref[h] = jnp.exp2(s_ref[h] - shift_ref[h:h + 1, :]).astype(BF16)
        for h in range(B_HEADS):
            pv = jnp.dot(vt_ref[kj, head_rows(h), :], p_ref[h], preferred_element_type=F32)
            if own:
                acc_ref[head_rows(h), :] = pv
            else:
                acc_ref[head_rows(h), :] = alpha_ref[h:h + 1, :] * acc_ref[head_rows(h), :] + pv

    def attend_all(attend):
        attend(qi, True)

        def past_blocks(i, carry):
            attend(2 * i, False)
            attend(2 * i + 1, False)
            return carry

        lax.fori_loop(0, lax.shift_right_logical(qi, 1), past_blocks, 0)
        pl.when((qi & 1) == 1)(lambda: attend(qi - 1, False))

    bound = ctl_ref[0]
    use_fixed_shift = ctl_ref[1] > 0.5
    pl.when(use_fixed_shift)(lambda: attend_all(attend_fixed_shift))
    pl.when(jnp.logical_not(use_fixed_shift))(lambda: attend_all(attend_running_max))

    for h in range(B_HEADS):
        num = acc_ref[pl.ds(h * VT_ROWS, HEAD_DIM), :]
        den = acc_ref[pl.ds(h * VT_ROWS + HEAD_DIM, 1), :]
        out_ref[pl.ds(h * HEAD_DIM, HEAD_DIM), :] = num / den
    o_ref[...] = out_ref[...].T.astype(BF16)


def _moba(ctl, qt, kblk, vtblk, kmbd):
    b, w, s = qt.shape
    nb = s // MOBA_BLOCK
    stat = pltpu.VMEM((2 * SUBLANES, MOBA_BLOCK), F32)
    return pl.pallas_call(
        _moba_body,
        grid=(b, nb),
        in_specs=[
            pl.BlockSpec(memory_space=pltpu.SMEM),
            pl.BlockSpec((None, w, MOBA_BLOCK), lambda i, j: (i, 0, j)),
            pl.BlockSpec((None, nb, MOBA_BLOCK, w), lambda i, j: (i, 0, 0, 0)),
            pl.BlockSpec((None, nb, B_HEADS * VT_ROWS, MOBA_BLOCK), lambda i, j: (i, 0, 0, 0)),
            pl.BlockSpec((None, B_HEADS * nb, w), lambda i, j: (i, 0, 0)),
        ],
        out_specs=pl.BlockSpec((None, MOBA_BLOCK, w), lambda i, j: (i, j, 0)),
        out_shape=jax.ShapeDtypeStruct((b, s, w), BF16),
        scratch_shapes=[pltpu.VMEM((B_HEADS * nb, MOBA_BLOCK), F32),
                        pltpu.VMEM((B_HEADS, LANES, MOBA_BLOCK), BF16),
                        pltpu.VMEM((B_HEADS, MOBA_BLOCK, MOBA_BLOCK), F32),
                        pltpu.VMEM((B_HEADS, MOBA_BLOCK, MOBA_BLOCK), BF16),
                        stat, stat, stat,
                        pltpu.VMEM((B_HEADS * VT_ROWS, MOBA_BLOCK), F32),
                        pltpu.VMEM((w, MOBA_BLOCK), F32)],
        compiler_params=_params(("arbitrary", "arbitrary"), 56),
        name="moba",
    )(ctl, qt, kblk, vtblk, kmbd)


def _row(v):
    return v.reshape(1, -1).astype(F32)


def _tiled_row(v, reps):
    return jnp.tile(v.astype(F32), reps).reshape(1, -1)


def kernel(x, mem, positions, ffn1_norm_g, ffn1_w_gate, ffn1_w_up, ffn1_w_down, mix_norm_g, mem_norm_g, w_mem_kv, mem_q_norm_g, mem_k_norm_g, w_o, ffn2_norm_g, ffn2_w_gate, ffn2_w_up, ffn2_w_down, a_w_in, a_dw_kernel, a_dw_bias, a_ln_g, a_ln_b, kv_norm_g, w_kv, k_norm_g, b_w_in, b_q_norm_g):
    b, s, d = x.shape
    t = b * s
    nb = s // MOBA_BLOCK
    seg_mem = _seg_mean_matrix(MEM_WIDTH)
    seg_primary = _seg_mean_matrix(PRIMARY_WIDTH)

    def ffn(h, norm_g, wg, wu, wd, layer, mix=None):
        out = _ffn(h.reshape(t, d), _row(norm_g[layer]), wg[layer], wu[layer], wd[layer], mix)
        return out.reshape(b, s, d)

    def memkv(layer):
        return _memkv(mem, _row(mem_norm_g[layer]), w_mem_kv[layer].astype(BF16),
                      _tiled_row(mem_k_norm_g[layer], MEM_HEADS), seg_mem)

    h = ffn(x, ffn1_norm_g, ffn1_w_gate, ffn1_w_up, ffn1_w_down, 0)
    kt0, v0 = memkv(0)
    dw = jnp.repeat(a_dw_kernel[0].reshape(CONV_WIDTH, CONV_CH), SUBLANES, axis=0)
    dw = dw.reshape(CONV_WIDTH * SUBLANES, CONV_CH // LANES, LANES).transpose(1, 0, 2)
    h = _mixer_a(h, _row(mix_norm_g[0]), a_w_in[0].astype(BF16), dw, _row(a_dw_bias[0]),
                 _row(a_ln_g[0]), _row(a_ln_b[0]), kt0, v0,
                 _tiled_row(mem_q_norm_g[0], MEM_HEADS), seg_mem, w_o[0].astype(BF16))
    h = ffn(h, ffn2_norm_g, ffn2_w_gate, ffn2_w_up, ffn2_w_down, 0)

    inv_freq = 1.0 / (ROPE_THETA ** (jnp.arange(0, ROPE_DIM, 2, dtype=F32) / ROPE_DIM))
    invf = jnp.concatenate([inv_freq, inv_freq]).reshape(ROPE_DIM, 1)
    cos, sin = _rope_tables(positions.reshape(b, 1, s), invf)
    k, vt, km = _shared_kv(h, _row(kv_norm_g), w_kv.astype(BF16),
                           _tiled_row(k_norm_g, B_HEADS), seg_primary, cos, sin)
    kblk = k.reshape(b, nb, MOBA_BLOCK, PRIMARY_WIDTH)
    kmh = km.reshape(b, nb, B_HEADS, HEAD_DIM).transpose(0, 2, 1, 3)
    eye = jnp.eye(B_HEADS, dtype=F32)
    kmbd = (kmh[:, :, :, None, :] * eye[None, :, None, :, None]).reshape(
        b, B_HEADS * nb, PRIMARY_WIDTH).astype(BF16)

    h = ffn(h, ffn1_norm_g, ffn1_w_gate, ffn1_w_up, ffn1_w_down, 1)
    kt1, v1 = memkv(1)
    qt, mem_out = _proj_b(h, _row(mix_norm_g[1]), b_w_in[0].astype(BF16),
                          _tiled_row(b_q_norm_g[0], B_HEADS), seg_primary, cos, sin, kt1, v1,
                          _tiled_row(mem_q_norm_g[1], MEM_HEADS), seg_mem)
    score_bound = (SCORE_BOUND_PER_GAIN * jnp.max(jnp.abs(b_q_norm_g[0]))
                   * jnp.max(jnp.abs(k_norm_g))).astype(F32)
    ctl = jnp.stack([score_bound, (score_bound <= MAX_FIXED_SHIFT).astype(F32)])
    prim = _moba(ctl, qt, kblk, vt, kmbd)
    mix = (prim.reshape(t, PRIMARY_WIDTH), mem_out.reshape(t, MEM_WIDTH), w_o[1].astype(BF16))
    return ffn(h, ffn2_norm_g, ffn2_w_gate, ffn2_w_up, ffn2_w_down, 1, mix)
```

```python
import functools

import numpy as np
import jax
import jax.numpy as jnp
from jax import lax
from jax.experimental import pallas as pl
from jax.experimental.pallas import tpu as pltpu

F32 = jnp.float32
BF16 = jnp.bfloat16

D_MODEL = 1024
HEAD_DIM = 64
MEM_HEADS = 4
MEM_WIDTH = MEM_HEADS * HEAD_DIM
PRIMARY_WIDTH = D_MODEL - MEM_WIDTH
B_HEADS = PRIMARY_WIDTH // HEAD_DIM
CONV_CH = PRIMARY_WIDTH
CONV_WIDTH = 31
MOBA_BLOCK = 256
MOBA_TOPK = 3
ROPE_THETA = 500000.0
ROPE_DIM = HEAD_DIM // 4
EPS = 1e-6
SCALE = HEAD_DIM ** -0.5
LOG2E = float(np.log2(np.e))

LANES = 128
SUBLANES = 8
HALO = 32
CONV_ROWS = 64
FFN_CHUNK = 256
MASKED = -1e30
VT_ROWS = HEAD_DIM + 16
SCORE_BOUND_PER_GAIN = 1.05 * HEAD_DIM * SCALE * LOG2E
MAX_FIXED_SHIFT = 60.0
MIB = 1024 * 1024


def _params(semantics, vmem_mib):
    return pltpu.CompilerParams(dimension_semantics=semantics,
                                vmem_limit_bytes=vmem_mib * MIB)


def _rms(x, g):
    ms = jnp.mean(x * x, axis=-1, keepdims=True)
    return x * lax.rsqrt(ms + EPS) * g


def _head_rms(x, g, seg_mean):
    ms = jnp.dot((x * x).astype(BF16), seg_mean, preferred_element_type=F32)
    return x * lax.rsqrt(ms + EPS) * g


def _seg_mean_matrix(width):
    idx = np.arange(width) // HEAD_DIM
    return jnp.asarray((idx[:, None] == idx[None, :]).astype(np.float32) / HEAD_DIM, dtype=BF16)


def _rope(x, cos, sin):
    lane = lax.broadcasted_iota(jnp.int32, (1, LANES), 1) % HEAD_DIM
    first_half = lane < (ROPE_DIM // 2)
    outs = []
    for c in range(x.shape[1] // LANES):
        xc = x[:, c * LANES:(c + 1) * LANES]
        partner = jnp.where(first_half,
                            pltpu.roll(xc, LANES - ROPE_DIM // 2, 1),
                            pltpu.roll(xc, ROPE_DIM // 2, 1))
        outs.append(xc * cos + partner * sin)
    return jnp.concatenate(outs, axis=1)


def _mem_attention(qm, kt, v, qg, seg_mean):
    qn = _head_rms(qm, qg, seg_mean) * SCALE
    lane_head = lax.broadcasted_iota(jnp.int32, (1, MEM_WIDTH), 1) // HEAD_DIM
    out = jnp.zeros(qm.shape, F32)
    for h in range(MEM_HEADS):
        qh = jnp.where(lane_head == h, qn, 0.0).astype(BF16)
        s = jnp.dot(qh, kt, preferred_element_type=F32)
        m = jnp.max(s, axis=-1, keepdims=True)
        p = jnp.exp(s - m)
        l = jnp.sum(p, axis=-1, keepdims=True)
        vh = jnp.where(lane_head == h, v, jnp.zeros_like(v))
        out = out + jnp.dot(p.astype(BF16), vh, preferred_element_type=F32) / l
    return out


def _ffn_body(has_mix, *refs):
    if has_mix:
        prim_ref, mem_ref, wo_ref, *refs = refs
    x_ref, g_ref, wg_ref, wu_ref, wd_ref, o_ref, hmid_ref = refs
    x = x_ref[...]
    if has_mix:
        x = (x + jnp.dot(prim_ref[...], wo_ref[0:PRIMARY_WIDTH, :], preferred_element_type=F32)
             + jnp.dot(mem_ref[...], wo_ref[PRIMARY_WIDTH:, :], preferred_element_type=F32))
    xn = _rms(x, g_ref[...]).astype(BF16)
    for c in range(wg_ref.shape[1] // FFN_CHUNK):
        cols = pl.ds(c * FFN_CHUNK, FFN_CHUNK)
        gate = jnp.dot(xn, wg_ref[:, cols], preferred_element_type=F32)
        up = jnp.dot(xn, wu_ref[:, cols], preferred_element_type=F32)
        hmid_ref[:, cols] = (gate * jax.nn.sigmoid(gate) * up).astype(BF16)
    o_ref[...] = x + 0.5 * jnp.dot(hmid_ref[...], wd_ref[...], preferred_element_type=F32)


def _ffn(h2d, g, wg, wu, wd, layer, mix=None, *, tm=512):
    t, d = h2d.shape
    f = wg.shape[2]
    resident = dict(pipeline_mode=pl.Buffered(1))
    mix_args, mix_specs = (), []
    if mix is not None:
        mix_args = mix
        mix_specs = [
            pl.BlockSpec((tm, PRIMARY_WIDTH), lambda i: (i, 0)),
            pl.BlockSpec((tm, MEM_WIDTH), lambda i: (i, 0)),
            pl.BlockSpec((d, d), lambda i: (0, 0), **resident),
        ]
    return pl.pallas_call(
        functools.partial(_ffn_body, mix is not None),
        grid=(t // tm,),
        in_specs=mix_specs + [
            pl.BlockSpec((tm, d), lambda i: (i, 0)),
            pl.BlockSpec((1, d), lambda i: (0, 0)),
            pl.BlockSpec((None, d, f), lambda i: (layer, 0, 0), **resident),
            pl.BlockSpec((None, d, f), lambda i: (layer, 0, 0), **resident),
            pl.BlockSpec((None, f, d), lambda i: (layer, 0, 0), **resident),
        ],
        out_specs=pl.BlockSpec((tm, d), lambda i: (i, 0)),
        out_shape=jax.ShapeDtypeStruct((t, d), F32),
        scratch_shapes=[pltpu.VMEM((tm, f), BF16)],
        compiler_params=_params(("arbitrary",), 56),
        name="ffn_mix" if mix is not None else "ffn",
    )(*mix_args, h2d, g, wg, wu, wd)


def _memkv_body(mem_ref, g_ref, w_ref, kg_ref, seg_ref, kt_ref, v_ref):
    mn = _rms(mem_ref[...], g_ref[...]).astype(BF16)
    kv = jnp.dot(mn, w_ref[...], preferred_element_type=F32)
    k = _head_rms(kv[:, :MEM_WIDTH], kg_ref[...], seg_ref[...])
    kt_ref[...] = k.T.astype(BF16)
    v_ref[...] = kv[:, MEM_WIDTH:].astype(BF16)


def _memkv(mem, g, w, kg, seg):
    b, m, d = mem.shape
    return pl.pallas_call(
        _memkv_body,
        grid=(b,),
        in_specs=[
            pl.BlockSpec((None, m, d), lambda i: (i, 0, 0)),
            pl.BlockSpec((1, d), lambda i: (0, 0)),
            pl.BlockSpec((d, 2 * MEM_WIDTH), lambda i: (0, 0)),
            pl.BlockSpec((1, MEM_WIDTH), lambda i: (0, 0)),
            pl.BlockSpec((MEM_WIDTH, MEM_WIDTH), lambda i: (0, 0)),
        ],
        out_specs=[
            pl.BlockSpec((None, MEM_WIDTH, m), lambda i: (i, 0, 0)),
            pl.BlockSpec((None, m, MEM_WIDTH), lambda i: (i, 0, 0)),
        ],
        out_shape=[jax.ShapeDtypeStruct((b, MEM_WIDTH, m), BF16),
                   jax.ShapeDtypeStruct((b, m, MEM_WIDTH), BF16)],
        compiler_params=_params(("arbitrary",), 32),
        name="memkv",
    )(mem, g, w, kg, seg)


def _mixer_a_body(tm, h_ref, g_ref, win_ref, dw_ref, db_ref, lng_ref, lnb_ref,
                  kt_ref, v_ref, qg_ref, seg_ref, wo_ref, o_ref, buf_ref, conv_ref):
    n_strips = CONV_CH // LANES

    @pl.when(pl.program_id(1) == 0)
    def _():
        buf_ref[0, :, 0:HALO, :] = jnp.zeros((n_strips, HALO, LANES), F32)

    h = h_ref[...]
    hn = _rms(h, g_ref[...]).astype(BF16)
    u = jnp.dot(hn, win_ref[...], preferred_element_type=F32)
    glu = u[:, :CONV_CH] * jax.nn.sigmoid(u[:, CONV_CH:2 * CONV_CH])

    groups = CONV_ROWS // SUBLANES
    for cb in range(n_strips):
        buf_ref[0, cb, HALO:HALO + tm, :] = glu[:, cb * LANES:(cb + 1) * LANES]
        x_strip = buf_ref[0, cb]
        for r in range(1, SUBLANES):
            buf_ref[r, cb] = pltpu.roll(x_strip, r, 0)

        def conv_chunk(c, carry, cb=cb):
            base = pl.multiple_of(c * CONV_ROWS, CONV_ROWS)
            accs = [jnp.zeros((SUBLANES, LANES), F32)] * groups
            for j in range(CONV_WIDTH):
                k = CONV_WIDTH - 1 - j
                start = base + (HALO - SUBLANES * (j // SUBLANES))
                w8 = dw_ref[cb, k * SUBLANES:(k + 1) * SUBLANES, :]
                xs = buf_ref[j % SUBLANES, cb, pl.ds(start, CONV_ROWS), :]
                accs = [accs[i] + w8 * xs[i * SUBLANES:(i + 1) * SUBLANES] for i in range(groups)]
            conv_ref[pl.ds(base, CONV_ROWS), pl.ds(cb * LANES, LANES)] = jnp.concatenate(accs, axis=0)
            return carry

        lax.fori_loop(0, tm // CONV_ROWS, conv_chunk, 0, unroll=2)
    buf_ref[0, :, 0:HALO, :] = buf_ref[0, :, tm:tm + HALO, :]

    c = conv_ref[...] + db_ref[...]
    mu = jnp.mean(c, axis=-1, keepdims=True)
    xc = c - mu
    var = jnp.mean(xc * xc, axis=-1, keepdims=True)
    y = xc * lax.rsqrt(var + EPS) * lng_ref[...] + lnb_ref[...]
    prim = (y * jax.nn.sigmoid(y)).astype(BF16)

    mem = _mem_attention(u[:, 2 * CONV_CH:], kt_ref[...], v_ref[...], qg_ref[...], seg_ref[...])
    o_ref[...] = (h
                  + jnp.dot(prim, wo_ref[0:CONV_CH, :], preferred_element_type=F32)
                  + jnp.dot(mem.astype(BF16), wo_ref[CONV_CH:, :], preferred_element_type=F32))


def _mixer_a(h, g, win, dw, db, lng, lnb, kt, v, qg, seg, wo, *, tm=512):
    b, s, d = h.shape
    m = kt.shape[2]
    const = lambda i, j: (0, 0)
    return pl.pallas_call(
        functools.partial(_mixer_a_body, tm),
        grid=(b, s // tm),
        in_specs=[
            pl.BlockSpec((None, tm, d), lambda i, j: (i, j, 0)),
            pl.BlockSpec((1, d), const),
            pl.BlockSpec(win.shape, const),
            pl.BlockSpec(dw.shape, lambda i, j: (0, 0, 0)),
            pl.BlockSpec((1, CONV_CH), const),
            pl.BlockSpec((1, CONV_CH), const),
            pl.BlockSpec((1, CONV_CH), const),
            pl.BlockSpec((None, MEM_WIDTH, m), lambda i, j: (i, 0, 0)),
            pl.BlockSpec((None, m, MEM_WIDTH), lambda i, j: (i, 0, 0)),
            pl.BlockSpec((1, MEM_WIDTH), const),
            pl.BlockSpec((MEM_WIDTH, MEM_WIDTH), const),
            pl.BlockSpec((d, d), const),
        ],
        out_specs=pl.BlockSpec((None, tm, d), lambda i, j: (i, j, 0)),
        out_shape=jax.ShapeDtypeStruct((b, s, d), F32),
        scratch_shapes=[pltpu.VMEM((SUBLANES, CONV_CH // LANES, tm + HALO, LANES), F32),
                        pltpu.VMEM((tm, CONV_CH), F32)],
        compiler_params=_params(("arbitrary", "arbitrary"), 48),
        name="mixer_a",
    )(h, g, win, dw, db, lng, lnb, kt, v, qg, seg, wo)


def _rope_tab_body(pos_ref, invf_ref, cos_ref, sin_ref, cst_ref):
    tm = pos_ref.shape[1]
    ang = invf_ref[...] * pos_ref[...].astype(F32)
    c = jnp.cos(ang)
    s = jnp.sin(ang)
    half = ROPE_DIM // 2
    cst_ref[...] = jnp.concatenate([c[:half], s[:half]], axis=0)
    row = lax.broadcasted_iota(jnp.int32, ang.shape, 0)
    s = jnp.where(row < ROPE_DIM // 2, -s, s)
    rest = HEAD_DIM - ROPE_DIM
    ones = jnp.ones((rest, tm), F32)
    zeros = jnp.zeros((rest, tm), F32)
    cos_ref[...] = jnp.concatenate([c, ones, c, ones], axis=0).T
    sin_ref[...] = jnp.concatenate([s, zeros, s, zeros], axis=0).T


def _rope_tables(pos3, invf, *, tm=512):
    b, _, s = pos3.shape
    return pl.pallas_call(
        _rope_tab_body,
        grid=(b, s // tm),
        in_specs=[
            pl.BlockSpec((None, 1, tm), lambda i, j: (i, 0, j)),
            pl.BlockSpec((ROPE_DIM, 1), lambda i, j: (0, 0)),
        ],
        out_specs=[pl.BlockSpec((None, tm, LANES), lambda i, j: (i, j, 0)),
                   pl.BlockSpec((None, tm, LANES), lambda i, j: (i, j, 0)),
                   pl.BlockSpec((None, ROPE_DIM, tm), lambda i, j: (i, 0, j))],
        out_shape=[jax.ShapeDtypeStruct((b, s, LANES), F32),
                   jax.ShapeDtypeStruct((b, s, LANES), F32),
                   jax.ShapeDtypeStruct((b, ROPE_DIM, s), F32)],
        compiler_params=_params(("arbitrary", "arbitrary"), 32),
        name="rope_tab",
    )(pos3, invf)


def _shared_kv_body(tm, h_ref, g_ref, w_ref, kg_ref, seg_ref, cos_ref, sin_ref,
                    k_ref, vt_ref, km_ref):
    hn = _rms(h_ref[...], g_ref[...]).astype(BF16)
    kv = jnp.dot(hn, w_ref[...], preferred_element_type=F32)
    kn = _head_rms(kv[:, :PRIMARY_WIDTH], kg_ref[...], seg_ref[...])
    kr = _rope(kn, cos_ref[...], sin_ref[...])
    k_ref[...] = kr.astype(BF16)
    v = kv[:, PRIMARY_WIDTH:]
    ones = jnp.ones((VT_ROWS - HEAD_DIM, MOBA_BLOCK), BF16)
    for blk in range(tm // MOBA_BLOCK):
        rows = slice(blk * MOBA_BLOCK, (blk + 1) * MOBA_BLOCK)
        km_ref[blk] = jnp.mean(kr[rows], axis=0, keepdims=True)
        vt = v[rows].T.astype(BF16)
        for h in range(B_HEADS):
            vt_ref[blk, h * VT_ROWS:h * VT_ROWS + HEAD_DIM, :] = vt[h * HEAD_DIM:(h + 1) * HEAD_DIM]
            vt_ref[blk, h * VT_ROWS + HEAD_DIM:(h + 1) * VT_ROWS, :] = ones


def _shared_kv(h, g, w, kg, seg, cos, sin, *, tm=512):
    b, s, d = h.shape
    nb = s // MOBA_BLOCK
    bpt = tm // MOBA_BLOCK
    const = lambda i, j: (0, 0)
    return pl.pallas_call(
        functools.partial(_shared_kv_body, tm),
        grid=(b, s // tm),
        in_specs=[
            pl.BlockSpec((None, tm, d), lambda i, j: (i, j, 0)),
            pl.BlockSpec((1, d), const),
            pl.BlockSpec(w.shape, const),
            pl.BlockSpec((1, PRIMARY_WIDTH), const),
            pl.BlockSpec((PRIMARY_WIDTH, PRIMARY_WIDTH), const),
            pl.BlockSpec((None, tm, LANES), lambda i, j: (i, j, 0)),
            pl.BlockSpec((None, tm, LANES), lambda i, j: (i, j, 0)),
        ],
        out_specs=[
            pl.BlockSpec((None, tm, PRIMARY_WIDTH), lambda i, j: (i, j, 0)),
            pl.BlockSpec((None, bpt, B_HEADS * VT_ROWS, MOBA_BLOCK), lambda i, j: (i, j, 0, 0)),
            pl.BlockSpec((None, bpt, 1, PRIMARY_WIDTH), lambda i, j: (i, j, 0, 0)),
        ],
        out_shape=[jax.ShapeDtypeStruct((b, s, PRIMARY_WIDTH), BF16),
                   jax.ShapeDtypeStruct((b, nb, B_HEADS * VT_ROWS, MOBA_BLOCK), BF16),
                   jax.ShapeDtypeStruct((b, nb, 1, PRIMARY_WIDTH), F32)],
        compiler_params=_params(("arbitrary", "arbitrary"), 48),
        name="shared_kv",
    )(h, g, w, kg, seg, cos, sin)


def _proj_b_body(h_ref, g_ref, win_ref, qg_ref, seg_ref, cst_ref,
                 kt_ref, v_ref, mqg_ref, mseg_ref, qt_ref, mem_ref):
    hn = _rms(h_ref[...], g_ref[...]).astype(BF16)
    u = jnp.dot(hn, win_ref[...], preferred_element_type=F32)
    qt = u[:, :PRIMARY_WIDTH].T
    ms = jnp.dot(seg_ref[...], (qt * qt).astype(BF16), preferred_element_type=F32)
    inv = lax.rsqrt(ms + EPS) * (SCALE * LOG2E)
    half = ROPE_DIM // 2
    cos = cst_ref[0:half, :]
    sin = cst_ref[half:ROPE_DIM, :]
    for h in range(B_HEADS):
        rows = slice(h * HEAD_DIM, (h + 1) * HEAD_DIM)
        qh = qt[rows] * inv[h:h + 1, :] * qg_ref[rows, :]
        x1, x2 = qh[:half], qh[half:ROPE_DIM]
        rot = jnp.concatenate([x1 * cos - x2 * sin, x2 * cos + x1 * sin], axis=0)
        qt_ref[h * HEAD_DIM:h * HEAD_DIM + ROPE_DIM, :] = rot.astype(BF16)
        qt_ref[h * HEAD_DIM + ROPE_DIM:(h + 1) * HEAD_DIM, :] = qh[ROPE_DIM:].astype(BF16)
    mem = _mem_attention(u[:, PRIMARY_WIDTH:], kt_ref[...], v_ref[...], mqg_ref[...], mseg_ref[...])
    mem_ref[...] = mem.astype(BF16)


def _proj_b(h, g, win, qg, seg, cst, kt, v, mqg, mseg, *, tm=512):
    b, s, d = h.shape
    m = kt.shape[2]
    const = lambda i, j: (0, 0)
    return pl.pallas_call(
        _proj_b_body,
        grid=(b, s // tm),
        in_specs=[
            pl.BlockSpec((None, tm, d), lambda i, j: (i, j, 0)),
            pl.BlockSpec((1, d), const),
            pl.BlockSpec(win.shape, const),
            pl.BlockSpec((PRIMARY_WIDTH, tm), const),
            pl.BlockSpec((2 * SUBLANES, PRIMARY_WIDTH), const),
            pl.BlockSpec((None, ROPE_DIM, tm), lambda i, j: (i, 0, j)),
            pl.BlockSpec((None, MEM_WIDTH, m), lambda i, j: (i, 0, 0)),
            pl.BlockSpec((None, m, MEM_WIDTH), lambda i, j: (i, 0, 0)),
            pl.BlockSpec((1, MEM_WIDTH), const),
            pl.BlockSpec((MEM_WIDTH, MEM_WIDTH), const),
        ],
        out_specs=[pl.BlockSpec((None, PRIMARY_WIDTH, tm), lambda i, j: (i, 0, j)),
                   pl.BlockSpec((None, tm, MEM_WIDTH), lambda i, j: (i, j, 0))],
        out_shape=[jax.ShapeDtypeStruct((b, PRIMARY_WIDTH, s), BF16),
                   jax.ShapeDtypeStruct((b, s, MEM_WIDTH), BF16)],
        compiler_params=_params(("arbitrary", "arbitrary"), 48),
        name="proj_b",
    )(h, g, win, qg, seg, cst, kt, v, mqg, mseg)


def _moba_body(ctl_ref, qt_ref, k_ref, vt_ref, km_ref, o_ref,
               bias_ref, qh_ref, s_ref, p_ref, m_ref, shift_ref, alpha_ref, acc_ref, out_ref):
    qi = pl.program_id(1)
    nb = km_ref.shape[0] // B_HEADS
    tq = qt_ref.shape[1]
    qt = qt_ref[...]

    gate = jnp.dot(km_ref[...], qt, preferred_element_type=F32)
    blk = lax.broadcasted_iota(jnp.int32, (nb, tq), 0)
    for h in range(B_HEADS):
        g = gate[h * nb:(h + 1) * nb, :]
        rank = jnp.zeros((nb, tq), F32)
        for m in range(nb):
            gm = g[m:m + 1, :]
            valid = (m < qi).astype(F32)
            rank = rank + jnp.where(m < blk, jnp.where(gm >= g, valid, 0.0),
                                    jnp.where(gm > g, valid, 0.0))
        bias_ref[h * nb:(h + 1) * nb, :] = jnp.where(rank < MOBA_TOPK, 0.0, MASKED)

    pair_row = lax.broadcasted_iota(jnp.int32, (LANES, tq), 0)
    for h in range(B_HEADS):
        pair = h // 2
        in_head = (pair_row // HEAD_DIM) == (h % 2)
        qh_ref[h] = jnp.where(in_head, qt[pair * LANES:(pair + 1) * LANES, :],
                              jnp.zeros((LANES, tq), BF16))

    key_pos = lax.broadcasted_iota(jnp.int32, (MOBA_BLOCK, tq), 0)
    q_pos = lax.broadcasted_iota(jnp.int32, (MOBA_BLOCK, tq), 1)
    causal = key_pos <= q_pos

    def scores(kj, h, own):
        lanes = pl.ds((h // 2) * LANES, LANES)
        s = jnp.dot(k_ref[kj, :, lanes], qh_ref[h], preferred_element_type=F32)
        return jnp.where(causal, s, MASKED) if own else s

    def head_rows(h):
        return pl.ds(h * VT_ROWS, VT_ROWS)

    def attend_fixed_shift(kj, own):
        for h in range(B_HEADS):
            shift = bound if own else bound - bias_ref[pl.ds(h * nb + kj, 1), :]
            p_ref[h] = jnp.exp2(scores(kj, h, own) - shift).astype(BF16)
        for h in range(B_HEADS):
            pv = jnp.dot(vt_ref[kj, head_rows(h), :], p_ref[h], preferred_element_type=F32)
            if own:
                acc_ref[head_rows(h), :] = pv
            else:
                acc_ref[head_rows(h), :] += pv

    def attend_running_max(kj, own):
        for h in range(B_HEADS):
            s_ref[h] = scores(kj, h, own)
        for h in range(B_HEADS):
            blk_max = jnp.max(s_ref[h], axis=0, keepdims=True)
            if own:
                m_ref[h:h + 1, :] = blk_max
                shift_ref[h:h + 1, :] = blk_max
            else:
                bias = bias_ref[pl.ds(h * nb + kj, 1), :]
                m_old = m_ref[h:h + 1, :]
                m_new = jnp.maximum(m_old, blk_max + bias)
                shift_ref[h:h + 1, :] = m_new - bias
                alpha_ref[h:h + 1, :] = jnp.exp2(m_old - m_new)
                m_ref[h:h + 1, :] = m_new
        for h in range(B_HEADS):
            p_ref[h] = jnp.exp2(s_ref[h] - shift_ref[h:h + 1, :]).astype(BF16)
        for h in range(B_HEADS):
            pv = jnp.dot(vt_ref[kj, head_rows(h), :], p_ref[h], preferred_element_type=F32)
            if own:
                acc_ref[head_rows(h), :] = pv
            else:
                acc_ref[head_rows(h), :] = alpha_ref[h:h + 1, :] * acc_ref[head_rows(h), :] + pv

    def attend_all(attend):
        attend(qi, True)

        def past_blocks(i, carry):
            attend(2 * i, False)
            attend(2 * i + 1, False)
            return carry

        lax.fori_loop(0, lax.shift_right_logical(qi, 1), past_blocks, 0)
        pl.when((qi & 1) == 1)(lambda: attend(qi - 1, False))

    bound = ctl_ref[0]
    use_fixed_shift = ctl_ref[1] > 0.5
    pl.when(use_fixed_shift)(lambda: attend_all(attend_fixed_shift))
    pl.when(jnp.logical_not(use_fixed_shift))(lambda: attend_all(attend_running_max))

    for h in range(B_HEADS):
        num = acc_ref[pl.ds(h * VT_ROWS, HEAD_DIM), :]
        den = acc_ref[pl.ds(h * VT_ROWS + HEAD_DIM, 1), :]
        out_ref[pl.ds(h * HEAD_DIM, HEAD_DIM), :] = num / den
    o_ref[...] = out_ref[...].T.astype(BF16)


def _moba(ctl, qt, kblk, vtblk, kmbd):
    b, w, s = qt.shape
    nb = s // MOBA_BLOCK
    stat = pltpu.VMEM((2 * SUBLANES, MOBA_BLOCK), F32)
    return pl.pallas_call(
        _moba_body,
        grid=(b, nb),
        in_specs=[
            pl.BlockSpec(memory_space=pltpu.SMEM),
            pl.BlockSpec((None, w, MOBA_BLOCK), lambda i, j: (i, 0, j)),
            pl.BlockSpec((None, nb, MOBA_BLOCK, w), lambda i, j: (i, 0, 0, 0)),
            pl.BlockSpec((None, nb, B_HEADS * VT_ROWS, MOBA_BLOCK), lambda i, j: (i, 0, 0, 0)),
            pl.BlockSpec((None, B_HEADS * nb, w), lambda i, j: (i, 0, 0)),
        ],
        out_specs=pl.BlockSpec((None, MOBA_BLOCK, w), lambda i, j: (i, j, 0)),
        out_shape=jax.ShapeDtypeStruct((b, s, w), BF16),
        scratch_shapes=[pltpu.VMEM((B_HEADS * nb, MOBA_BLOCK), F32),
                        pltpu.VMEM((B_HEADS, LANES, MOBA_BLOCK), BF16),
                        pltpu.VMEM((B_HEADS, MOBA_BLOCK, MOBA_BLOCK), F32),
                        pltpu.VMEM((B_HEADS, MOBA_BLOCK, MOBA_BLOCK), BF16),
                        stat, stat, stat,
                        pltpu.VMEM((B_HEADS * VT_ROWS, MOBA_BLOCK), F32),
                        pltpu.VMEM((w, MOBA_BLOCK), F32)],
        compiler_params=_params(("arbitrary", "arbitrary"), 56),
        name="moba",
    )(ctl, qt, kblk, vtblk, kmbd)


def _row(v):
    return v.reshape(1, -1).astype(F32)


def _tiled_row(v, reps):
    return jnp.tile(v.astype(F32), reps).reshape(1, -1)


def kernel(x, mem, positions, ffn1_norm_g, ffn1_w_gate, ffn1_w_up, ffn1_w_down, mix_norm_g, mem_norm_g, w_mem_kv, mem_q_norm_g, mem_k_norm_g, w_o, ffn2_norm_g, ffn2_w_gate, ffn2_w_up, ffn2_w_down, a_w_in, a_dw_kernel, a_dw_bias, a_ln_g, a_ln_b, kv_norm_g, w_kv, k_norm_g, b_w_in, b_q_norm_g):
    b, s, d = x.shape
    t = b * s
    nb = s // MOBA_BLOCK
    seg_mem = _seg_mean_matrix(MEM_WIDTH)
    seg_primary = _seg_mean_matrix(PRIMARY_WIDTH)

    def ffn(h, norm_g, wg, wu, wd, layer, mix=None):
        out = _ffn(h.reshape(t, d), _row(norm_g[layer]), wg, wu, wd, layer, mix)
        return out.reshape(b, s, d)

    def memkv(layer):
        return _memkv(mem, _row(mem_norm_g[layer]), w_mem_kv[layer].astype(BF16),
                      _tiled_row(mem_k_norm_g[layer], MEM_HEADS), seg_mem)

    h = ffn(x, ffn1_norm_g, ffn1_w_gate, ffn1_w_up, ffn1_w_down, 0)
    kt0, v0 = memkv(0)
    dw = jnp.repeat(a_dw_kernel[0].reshape(CONV_WIDTH, CONV_CH), SUBLANES, axis=0)
    dw = dw.reshape(CONV_WIDTH * SUBLANES, CONV_CH // LANES, LANES).transpose(1, 0, 2)
    h = _mixer_a(h, _row(mix_norm_g[0]), a_w_in[0].astype(BF16), dw, _row(a_dw_bias[0]),
                 _row(a_ln_g[0]), _row(a_ln_b[0]), kt0, v0,
                 _tiled_row(mem_q_norm_g[0], MEM_HEADS), seg_mem, w_o[0].astype(BF16))
    h = ffn(h, ffn2_norm_g, ffn2_w_gate, ffn2_w_up, ffn2_w_down, 0)

    inv_freq = 1.0 / (ROPE_THETA ** (jnp.arange(0, ROPE_DIM, 2, dtype=F32) / ROPE_DIM))
    invf = jnp.concatenate([inv_freq, inv_freq]).reshape(ROPE_DIM, 1)
    cos, sin, cst = _rope_tables(positions.reshape(b, 1, s), invf)
    k, vt, km = _shared_kv(h, _row(kv_norm_g), w_kv.astype(BF16),
                           _tiled_row(k_norm_g, B_HEADS), seg_primary, cos, sin)
    kblk = k.reshape(b, nb, MOBA_BLOCK, PRIMARY_WIDTH)
    kmh = km.reshape(b, nb, B_HEADS, HEAD_DIM).transpose(0, 2, 1, 3)
    eye = jnp.eye(B_HEADS, dtype=F32)
    kmbd = (kmh[:, :, :, None, :] * eye[None, :, None, :, None]).reshape(
        b, B_HEADS * nb, PRIMARY_WIDTH).astype(BF16)

    h = ffn(h, ffn1_norm_g, ffn1_w_gate, ffn1_w_up, ffn1_w_down, 1)
    kt1, v1 = memkv(1)
    proj_tm = 512
    q_gain_t = jnp.broadcast_to(jnp.tile(b_q_norm_g[0].astype(F32), B_HEADS)[:, None],
                                (PRIMARY_WIDTH, proj_tm))
    head_of = np.arange(PRIMARY_WIDTH) // HEAD_DIM
    seg_rows = jnp.asarray((np.arange(2 * SUBLANES)[:, None] == head_of[None, :]) / HEAD_DIM, dtype=BF16)
    qt, mem_out = _proj_b(h, _row(mix_norm_g[1]), b_w_in[0].astype(BF16), q_gain_t, seg_rows, cst,
                          kt1, v1, _tiled_row(mem_q_norm_g[1], MEM_HEADS), seg_mem, tm=proj_tm)
    score_bound = (SCORE_BOUND_PER_GAIN * jnp.max(jnp.abs(b_q_norm_g[0]))
                   * jnp.max(jnp.abs(k_norm_g))).astype(F32)
    ctl = jnp.stack([score_bound, (score_bound <= MAX_FIXED_SHIFT).astype(F32)])
    prim = _moba(ctl, qt, kblk, vt, kmbd)
    mix = (prim.reshape(t, PRIMARY_WIDTH), mem_out.reshape(t, MEM_WIDTH), w_o[1].astype(BF16))
    return ffn(h, ffn2_norm_g, ffn2_w_gate, ffn2_w_up, ffn2_w_down, 1, mix)
```

```python
import functools

import numpy as np
import jax
import jax.numpy as jnp
from jax import lax
from jax.experimental import pallas as pl
from jax.experimental.pallas import tpu as pltpu

F32 = jnp.float32
BF16 = jnp.bfloat16

D_MODEL = 1024
HEAD_DIM = 64
MEM_HEADS = 4
MEM_WIDTH = MEM_HEADS * HEAD_DIM
PRIMARY_WIDTH = D_MODEL - MEM_WIDTH
B_HEADS = PRIMARY_WIDTH // HEAD_DIM
CONV_CH = PRIMARY_WIDTH
CONV_WIDTH = 31
MOBA_BLOCK = 256
MOBA_TOPK = 3
ROPE_THETA = 500000.0
ROPE_DIM = HEAD_DIM // 4
EPS = 1e-6
SCALE = HEAD_DIM ** -0.5
LOG2E = float(np.log2(np.e))

LANES = 128
SUBLANES = 8
HALO = 32
CONV_ROWS = 64
FFN_CHUNK = 256
MASKED = -1e30
VT_ROWS = HEAD_DIM + 16
SCORE_BOUND_PER_GAIN = 1.05 * HEAD_DIM * SCALE * LOG2E
MAX_FIXED_SHIFT = 60.0
PAST_UNROLL = 2
MIB = 1024 * 1024


def _params(semantics, vmem_mib):
    return pltpu.CompilerParams(dimension_semantics=semantics,
                                vmem_limit_bytes=vmem_mib * MIB)


def _rms(x, g):
    ms = jnp.mean(x * x, axis=-1, keepdims=True)
    return x * lax.rsqrt(ms + EPS) * g


def _head_rms(x, g, seg_mean):
    ms = jnp.dot((x * x).astype(BF16), seg_mean, preferred_element_type=F32)
    return x * lax.rsqrt(ms + EPS) * g


def _seg_mean_matrix(width):
    idx = np.arange(width) // HEAD_DIM
    return jnp.asarray((idx[:, None] == idx[None, :]).astype(np.float32) / HEAD_DIM, dtype=BF16)


def _rope(x, cos, sin):
    lane = lax.broadcasted_iota(jnp.int32, (1, LANES), 1) % HEAD_DIM
    first_half = lane < (ROPE_DIM // 2)
    outs = []
    for c in range(x.shape[1] // LANES):
        xc = x[:, c * LANES:(c + 1) * LANES]
        partner = jnp.where(first_half,
                            pltpu.roll(xc, LANES - ROPE_DIM // 2, 1),
                            pltpu.roll(xc, ROPE_DIM // 2, 1))
        outs.append(xc * cos + partner * sin)
    return jnp.concatenate(outs, axis=1)


def _mem_attention(qm, kt, v, qg, seg_mean):
    qn = _head_rms(qm, qg, seg_mean) * SCALE
    lane_head = lax.broadcasted_iota(jnp.int32, (1, MEM_WIDTH), 1) // HEAD_DIM
    out = jnp.zeros(qm.shape, F32)
    for h in range(MEM_HEADS):
        qh = jnp.where(lane_head == h, qn, 0.0).astype(BF16)
        s = jnp.dot(qh, kt, preferred_element_type=F32)
        m = jnp.max(s, axis=-1, keepdims=True)
        p = jnp.exp(s - m)
        l = jnp.sum(p, axis=-1, keepdims=True)
        vh = jnp.where(lane_head == h, v, jnp.zeros_like(v))
        out = out + jnp.dot(p.astype(BF16), vh, preferred_element_type=F32) / l
    return out


def _ffn_body(has_mix, *refs):
    if has_mix:
        prim_ref, mem_ref, wo_ref, *refs = refs
    x_ref, g_ref, wg_ref, wu_ref, wd_ref, o_ref, hmid_ref = refs
    x = x_ref[...]
    if has_mix:
        x = (x + jnp.dot(prim_ref[...], wo_ref[0:PRIMARY_WIDTH, :], preferred_element_type=F32)
             + jnp.dot(mem_ref[...], wo_ref[PRIMARY_WIDTH:, :], preferred_element_type=F32))
    xn = _rms(x, g_ref[...]).astype(BF16)
    for c in range(wg_ref.shape[1] // FFN_CHUNK):
        cols = pl.ds(c * FFN_CHUNK, FFN_CHUNK)
        gate = jnp.dot(xn, wg_ref[:, cols], preferred_element_type=F32)
        up = jnp.dot(xn, wu_ref[:, cols], preferred_element_type=F32)
        hmid_ref[:, cols] = (gate * jax.nn.sigmoid(gate) * up).astype(BF16)
    o_ref[...] = x + 0.5 * jnp.dot(hmid_ref[...], wd_ref[...], preferred_element_type=F32)


def _ffn(h2d, g, wg, wu, wd, layer, mix=None, *, tm=512):
    t, d = h2d.shape
    f = wg.shape[2]
    resident = dict(pipeline_mode=pl.Buffered(1))
    mix_args, mix_specs = (), []
    if mix is not None:
        mix_args = mix
        mix_specs = [
            pl.BlockSpec((tm, PRIMARY_WIDTH), lambda i: (i, 0)),
            pl.BlockSpec((tm, MEM_WIDTH), lambda i: (i, 0)),
            pl.BlockSpec((d, d), lambda i: (0, 0), **resident),
        ]
    return pl.pallas_call(
        functools.partial(_ffn_body, mix is not None),
        grid=(t // tm,),
        in_specs=mix_specs + [
            pl.BlockSpec((tm, d), lambda i: (i, 0)),
            pl.BlockSpec((1, d), lambda i: (0, 0)),
            pl.BlockSpec((None, d, f), lambda i: (layer, 0, 0), **resident),
            pl.BlockSpec((None, d, f), lambda i: (layer, 0, 0), **resident),
            pl.BlockSpec((None, f, d), lambda i: (layer, 0, 0), **resident),
        ],
        out_specs=pl.BlockSpec((tm, d), lambda i: (i, 0)),
        out_shape=jax.ShapeDtypeStruct((t, d), F32),
        scratch_shapes=[pltpu.VMEM((tm, f), BF16)],
        compiler_params=_params(("arbitrary",), 56),
        name="ffn_mix" if mix is not None else "ffn",
    )(*mix_args, h2d, g, wg, wu, wd)


def _memkv_body(mem_ref, g_ref, w_ref, kg_ref, seg_ref, kt_ref, v_ref):
    mn = _rms(mem_ref[...], g_ref[...]).astype(BF16)
    kv = jnp.dot(mn, w_ref[...], preferred_element_type=F32)
    k = _head_rms(kv[:, :MEM_WIDTH], kg_ref[...], seg_ref[...])
    kt_ref[...] = k.T.astype(BF16)
    v_ref[...] = kv[:, MEM_WIDTH:].astype(BF16)


def _memkv(mem, g, w, kg, seg):
    b, m, d = mem.shape
    return pl.pallas_call(
        _memkv_body,
        grid=(b,),
        in_specs=[
            pl.BlockSpec((None, m, d), lambda i: (i, 0, 0)),
            pl.BlockSpec((1, d), lambda i: (0, 0)),
            pl.BlockSpec((d, 2 * MEM_WIDTH), lambda i: (0, 0)),
            pl.BlockSpec((1, MEM_WIDTH), lambda i: (0, 0)),
            pl.BlockSpec((MEM_WIDTH, MEM_WIDTH), lambda i: (0, 0)),
        ],
        out_specs=[
            pl.BlockSpec((None, MEM_WIDTH, m), lambda i: (i, 0, 0)),
            pl.BlockSpec((None, m, MEM_WIDTH), lambda i: (i, 0, 0)),
        ],
        out_shape=[jax.ShapeDtypeStruct((b, MEM_WIDTH, m), BF16),
                   jax.ShapeDtypeStruct((b, m, MEM_WIDTH), BF16)],
        compiler_params=_params(("arbitrary",), 32),
        name="memkv",
    )(mem, g, w, kg, seg)


def _mixer_a_body(tm, h_ref, g_ref, win_ref, dw_ref, db_ref, lng_ref, lnb_ref,
                  kt_ref, v_ref, qg_ref, seg_ref, wo_ref, o_ref, buf_ref, conv_ref):
    n_strips = CONV_CH // LANES

    @pl.when(pl.program_id(1) == 0)
    def _():
        buf_ref[0, :, 0:HALO, :] = jnp.zeros((n_strips, HALO, LANES), F32)

    h = h_ref[...]
    hn = _rms(h, g_ref[...]).astype(BF16)
    u = jnp.dot(hn, win_ref[...], preferred_element_type=F32)
    glu = u[:, :CONV_CH] * jax.nn.sigmoid(u[:, CONV_CH:2 * CONV_CH])

    groups = CONV_ROWS // SUBLANES
    for cb in range(n_strips):
        buf_ref[0, cb, HALO:HALO + tm, :] = glu[:, cb * LANES:(cb + 1) * LANES]
        x_strip = buf_ref[0, cb]
        for r in range(1, SUBLANES):
            buf_ref[r, cb] = pltpu.roll(x_strip, r, 0)

        def conv_chunk(c, carry, cb=cb):
            base = pl.multiple_of(c * CONV_ROWS, CONV_ROWS)
            accs = [jnp.zeros((SUBLANES, LANES), F32)] * groups
            for j in range(CONV_WIDTH):
                k = CONV_WIDTH - 1 - j
                start = base + (HALO - SUBLANES * (j // SUBLANES))
                w8 = dw_ref[cb, k * SUBLANES:(k + 1) * SUBLANES, :]
                xs = buf_ref[j % SUBLANES, cb, pl.ds(start, CONV_ROWS), :]
                accs = [accs[i] + w8 * xs[i * SUBLANES:(i + 1) * SUBLANES] for i in range(groups)]
            conv_ref[pl.ds(base, CONV_ROWS), pl.ds(cb * LANES, LANES)] = jnp.concatenate(accs, axis=0)
            return carry

        lax.fori_loop(0, tm // CONV_ROWS, conv_chunk, 0, unroll=2)
    buf_ref[0, :, 0:HALO, :] = buf_ref[0, :, tm:tm + HALO, :]

    c = conv_ref[...] + db_ref[...]
    mu = jnp.mean(c, axis=-1, keepdims=True)
    xc = c - mu
    var = jnp.mean(xc * xc, axis=-1, keepdims=True)
    y = xc * lax.rsqrt(var + EPS) * lng_ref[...] + lnb_ref[...]
    prim = (y * jax.nn.sigmoid(y)).astype(BF16)

    mem = _mem_attention(u[:, 2 * CONV_CH:], kt_ref[...], v_ref[...], qg_ref[...], seg_ref[...])
    o_ref[...] = (h
                  + jnp.dot(prim, wo_ref[0:CONV_CH, :], preferred_element_type=F32)
                  + jnp.dot(mem.astype(BF16), wo_ref[CONV_CH:, :], preferred_element_type=F32))


def _mixer_a(h, g, win, dw, db, lng, lnb, kt, v, qg, seg, wo, *, tm=512):
    b, s, d = h.shape
    m = kt.shape[2]
    const = lambda i, j: (0, 0)
    return pl.pallas_call(
        functools.partial(_mixer_a_body, tm),
        grid=(b, s // tm),
        in_specs=[
            pl.BlockSpec((None, tm, d), lambda i, j: (i, j, 0)),
            pl.BlockSpec((1, d), const),
            pl.BlockSpec(win.shape, const),
            pl.BlockSpec(dw.shape, lambda i, j: (0, 0, 0)),
            pl.BlockSpec((1, CONV_CH), const),
            pl.BlockSpec((1, CONV_CH), const),
            pl.BlockSpec((1, CONV_CH), const),
            pl.BlockSpec((None, MEM_WIDTH, m), lambda i, j: (i, 0, 0)),
            pl.BlockSpec((None, m, MEM_WIDTH), lambda i, j: (i, 0, 0)),
            pl.BlockSpec((1, MEM_WIDTH), const),
            pl.BlockSpec((MEM_WIDTH, MEM_WIDTH), const),
            pl.BlockSpec((d, d), const),
        ],
        out_specs=pl.BlockSpec((None, tm, d), lambda i, j: (i, j, 0)),
        out_shape=jax.ShapeDtypeStruct((b, s, d), F32),
        scratch_shapes=[pltpu.VMEM((SUBLANES, CONV_CH // LANES, tm + HALO, LANES), F32),
                        pltpu.VMEM((tm, CONV_CH), F32)],
        compiler_params=_params(("arbitrary", "arbitrary"), 48),
        name="mixer_a",
    )(h, g, win, dw, db, lng, lnb, kt, v, qg, seg, wo)


def _rope_tab_body(pos_ref, invf_ref, cos_ref, sin_ref, cst_ref):
    tm = pos_ref.shape[1]
    ang = invf_ref[...] * pos_ref[...].astype(F32)
    c = jnp.cos(ang)
    s = jnp.sin(ang)
    half = ROPE_DIM // 2
    cst_ref[...] = jnp.concatenate([c[:half], s[:half]], axis=0)
    row = lax.broadcasted_iota(jnp.int32, ang.shape, 0)
    s = jnp.where(row < ROPE_DIM // 2, -s, s)
    rest = HEAD_DIM - ROPE_DIM
    ones = jnp.ones((rest, tm), F32)
    zeros = jnp.zeros((rest, tm), F32)
    cos_ref[...] = jnp.concatenate([c, ones, c, ones], axis=0).T
    sin_ref[...] = jnp.concatenate([s, zeros, s, zeros], axis=0).T


def _rope_tables(pos3, invf, *, tm=512):
    b, _, s = pos3.shape
    return pl.pallas_call(
        _rope_tab_body,
        grid=(b, s // tm),
        in_specs=[
            pl.BlockSpec((None, 1, tm), lambda i, j: (i, 0, j)),
            pl.BlockSpec((ROPE_DIM, 1), lambda i, j: (0, 0)),
        ],
        out_specs=[pl.BlockSpec((None, tm, LANES), lambda i, j: (i, j, 0)),
                   pl.BlockSpec((None, tm, LANES), lambda i, j: (i, j, 0)),
                   pl.BlockSpec((None, ROPE_DIM, tm), lambda i, j: (i, 0, j))],
        out_shape=[jax.ShapeDtypeStruct((b, s, LANES), F32),
                   jax.ShapeDtypeStruct((b, s, LANES), F32),
                   jax.ShapeDtypeStruct((b, ROPE_DIM, s), F32)],
        compiler_params=_params(("arbitrary", "arbitrary"), 32),
        name="rope_tab",
    )(pos3, invf)


def _shared_kv_body(tm, h_ref, g_ref, w_ref, kg_ref, seg_ref, cos_ref, sin_ref,
                    k_ref, vt_ref, km_ref):
    hn = _rms(h_ref[...], g_ref[...]).astype(BF16)
    k = jnp.dot(hn, w_ref[:, 0:PRIMARY_WIDTH], preferred_element_type=F32)
    v = jnp.dot(hn, w_ref[:, PRIMARY_WIDTH:], preferred_element_type=F32)
    for c in range(PRIMARY_WIDTH // FFN_CHUNK):
        cols = pl.ds(c * FFN_CHUNK, FFN_CHUNK)
        kc = k[:, c * FFN_CHUNK:(c + 1) * FFN_CHUNK]
        kr = _rope(_head_rms(kc, kg_ref[:, cols], seg_ref[...]), cos_ref[...], sin_ref[...])
        k_ref[:, cols] = kr.astype(BF16)
        for blk in range(tm // MOBA_BLOCK):
            rows = slice(blk * MOBA_BLOCK, (blk + 1) * MOBA_BLOCK)
            km_ref[blk, :, cols] = jnp.mean(kr[rows], axis=0, keepdims=True)
    ones = jnp.ones((VT_ROWS - HEAD_DIM, MOBA_BLOCK), BF16)
    for blk in range(tm // MOBA_BLOCK):
        rows = slice(blk * MOBA_BLOCK, (blk + 1) * MOBA_BLOCK)
        vt = v[rows].T.astype(BF16)
        for h in range(B_HEADS):
            vt_ref[blk, h * VT_ROWS:h * VT_ROWS + HEAD_DIM, :] = vt[h * HEAD_DIM:(h + 1) * HEAD_DIM]
            vt_ref[blk, h * VT_ROWS + HEAD_DIM:(h + 1) * VT_ROWS, :] = ones


def _shared_kv(h, g, w, kg, seg, cos, sin, *, tm=512):
    b, s, d = h.shape
    nb = s // MOBA_BLOCK
    bpt = tm // MOBA_BLOCK
    const = lambda i, j: (0, 0)
    return pl.pallas_call(
        functools.partial(_shared_kv_body, tm),
        grid=(b, s // tm),
        in_specs=[
            pl.BlockSpec((None, tm, d), lambda i, j: (i, j, 0)),
            pl.BlockSpec((1, d), const),
            pl.BlockSpec(w.shape, const),
            pl.BlockSpec((1, PRIMARY_WIDTH), const),
            pl.BlockSpec((FFN_CHUNK, FFN_CHUNK), const),
            pl.BlockSpec((None, tm, LANES), lambda i, j: (i, j, 0)),
            pl.BlockSpec((None, tm, LANES), lambda i, j: (i, j, 0)),
        ],
        out_specs=[
            pl.BlockSpec((None, tm, PRIMARY_WIDTH), lambda i, j: (i, j, 0)),
            pl.BlockSpec((None, bpt, B_HEADS * VT_ROWS, MOBA_BLOCK), lambda i, j: (i, j, 0, 0)),
            pl.BlockSpec((None, bpt, 1, PRIMARY_WIDTH), lambda i, j: (i, j, 0, 0)),
        ],
        out_shape=[jax.ShapeDtypeStruct((b, s, PRIMARY_WIDTH), BF16),
                   jax.ShapeDtypeStruct((b, nb, B_HEADS * VT_ROWS, MOBA_BLOCK), BF16),
                   jax.ShapeDtypeStruct((b, nb, 1, PRIMARY_WIDTH), F32)],
        compiler_params=_params(("arbitrary", "arbitrary"), 48),
        name="shared_kv",
    )(h, g, w, kg, seg, cos, sin)


def _proj_b_body(h_ref, g_ref, win_ref, qg_ref, seg_ref, cst_ref,
                 kt_ref, v_ref, mqg_ref, mseg_ref, qt_ref, mem_ref):
    hn = _rms(h_ref[...], g_ref[...]).astype(BF16)
    u = jnp.dot(hn, win_ref[...], preferred_element_type=F32)
    qt = u[:, :PRIMARY_WIDTH].T
    ms = jnp.dot(seg_ref[...], (qt * qt).astype(BF16), preferred_element_type=F32)
    inv = lax.rsqrt(ms + EPS) * (SCALE * LOG2E)
    half = ROPE_DIM // 2
    cos = cst_ref[0:half, :]
    sin = cst_ref[half:ROPE_DIM, :]
    for h in range(B_HEADS):
        rows = slice(h * HEAD_DIM, (h + 1) * HEAD_DIM)
        qh = qt[rows] * inv[h:h + 1, :] * qg_ref[rows, :]
        x1, x2 = qh[:half], qh[half:ROPE_DIM]
        rot = jnp.concatenate([x1 * cos - x2 * sin, x2 * cos + x1 * sin], axis=0)
        qt_ref[h * HEAD_DIM:h * HEAD_DIM + ROPE_DIM, :] = rot.astype(BF16)
        qt_ref[h * HEAD_DIM + ROPE_DIM:(h + 1) * HEAD_DIM, :] = qh[ROPE_DIM:].astype(BF16)
    mem = _mem_attention(u[:, PRIMARY_WIDTH:], kt_ref[...], v_ref[...], mqg_ref[...], mseg_ref[...])
    mem_ref[...] = mem.astype(BF16)


def _proj_b(h, g, win, qg, seg, cst, kt, v, mqg, mseg, *, tm=512):
    b, s, d = h.shape
    m = kt.shape[2]
    const = lambda i, j: (0, 0)
    return pl.pallas_call(
        _proj_b_body,
        grid=(b, s // tm),
        in_specs=[
            pl.BlockSpec((None, tm, d), lambda i, j: (i, j, 0)),
            pl.BlockSpec((1, d), const),
            pl.BlockSpec(win.shape, const),
            pl.BlockSpec((PRIMARY_WIDTH, tm), const),
            pl.BlockSpec((2 * SUBLANES, PRIMARY_WIDTH), const),
            pl.BlockSpec((None, ROPE_DIM, tm), lambda i, j: (i, 0, j)),
            pl.BlockSpec((None, MEM_WIDTH, m), lambda i, j: (i, 0, 0)),
            pl.BlockSpec((None, m, MEM_WIDTH), lambda i, j: (i, 0, 0)),
            pl.BlockSpec((1, MEM_WIDTH), const),
            pl.BlockSpec((MEM_WIDTH, MEM_WIDTH), const),
        ],
        out_specs=[pl.BlockSpec((None, PRIMARY_WIDTH, tm), lambda i, j: (i, 0, j)),
                   pl.BlockSpec((None, tm, MEM_WIDTH), lambda i, j: (i, j, 0))],
        out_shape=[jax.ShapeDtypeStruct((b, PRIMARY_WIDTH, s), BF16),
                   jax.ShapeDtypeStruct((b, s, MEM_WIDTH), BF16)],
        compiler_params=_params(("arbitrary", "arbitrary"), 48),
        name="proj_b",
    )(h, g, win, qg, seg, cst, kt, v, mqg, mseg)


def _moba_body(ctl_ref, qt_ref, k_ref, vt_ref, km_ref, o_ref,
               bias_ref, qh_ref, s_ref, p_ref, m_ref, shift_ref, alpha_ref, acc_ref, out_ref):
    qi = pl.program_id(1)
    nb = km_ref.shape[0] // B_HEADS
    tq = qt_ref.shape[1]
    qt = qt_ref[...]

    def select_blocks():
        gate = jnp.dot(km_ref[...], qt, preferred_element_type=F32)
        blk = lax.broadcasted_iota(jnp.int32, (nb, tq), 0)
        for h in range(B_HEADS):
            g = gate[h * nb:(h + 1) * nb, :]
            rank = jnp.zeros((nb, tq), F32)
            for m in range(nb):
                gm = g[m:m + 1, :]
                valid = (m < qi).astype(F32)
                rank = rank + jnp.where(m < blk, jnp.where(gm >= g, valid, 0.0),
                                        jnp.where(gm > g, valid, 0.0))
            bias_ref[h * nb:(h + 1) * nb, :] = jnp.where(rank < MOBA_TOPK, 0.0, MASKED)

    pair_row = lax.broadcasted_iota(jnp.int32, (LANES, tq), 0)
    for h in range(B_HEADS):
        pair = h // 2
        in_head = (pair_row // HEAD_DIM) == (h % 2)
        qh_ref[h] = jnp.where(in_head, qt[pair * LANES:(pair + 1) * LANES, :],
                              jnp.zeros((LANES, tq), BF16))

    key_pos = lax.broadcasted_iota(jnp.int32, (MOBA_BLOCK, tq), 0)
    q_pos = lax.broadcasted_iota(jnp.int32, (MOBA_BLOCK, tq), 1)
    causal = key_pos <= q_pos

    def scores(kj, h, own):
        lanes = pl.ds((h // 2) * LANES, LANES)
        s = jnp.dot(k_ref[kj, :, lanes], qh_ref[h], preferred_element_type=F32)
        return jnp.where(causal, s, MASKED) if own else s

    def head_rows(h):
        return pl.ds(h * VT_ROWS, VT_ROWS)

    def attend_fixed_shift(kj, own):
        for h in range(B_HEADS):
            shift = bound if own else bound - bias_ref[pl.ds(h * nb + kj, 1), :]
            p_ref[h] = jnp.exp2(scores(kj, h, own) - shift).astype(BF16)
        for h in range(B_HEADS):
            pv = jnp.dot(vt_ref[kj, head_rows(h), :], p_ref[h], preferred_element_type=F32)
            if own:
                acc_ref[head_rows(h), :] = pv
            else:
                acc_ref[head_rows(h), :] += pv

    def attend_running_max(kj, own):
        for h in range(B_HEADS):
            s_ref[h] = scores(kj, h, own)
        for h in range(B_HEADS):
            blk_max = jnp.max(s_ref[h], axis=0, keepdims=True)
            if own:
                m_ref[h:h + 1, :] = blk_max
                shift_ref[h:h + 1, :] = blk_max
            else:
                bias = bias_ref[pl.ds(h * nb + kj, 1), :]
                m_old = m_ref[h:h + 1, :]
                m_new = jnp.maximum(m_old, blk_max + bias)
                shift_ref[h:h + 1, :] = m_new - bias
                alpha_ref[h:h + 1, :] = jnp.exp2(m_old - m_new)
                m_ref[h:h + 1, :] = m_new
        for h in range(B_HEADS):
            p_ref[h] = jnp.exp2(s_ref[h] - shift_ref[h:h + 1, :]).astype(BF16)
        for h in range(B_HEADS):
            pv = jnp.dot(vt_ref[kj, head_rows(h), :], p_ref[h], preferred_element_type=F32)
            if own:
                acc_ref[head_rows(h), :] = pv
            else:
                acc_ref[head_rows(h), :] = alpha_ref[h:h + 1, :] * acc_ref[head_rows(h), :] + pv

    def attend_all(attend):
        attend(qi, True)
        select_blocks()

        def past_blocks(i, carry):
            for u in range(PAST_UNROLL):
                attend(PAST_UNROLL * i + u, False)
            return carry

        groups = lax.shift_right_logical(qi, PAST_UNROLL.bit_length() - 1)
        lax.fori_loop(0, groups, past_blocks, 0)
        done = groups * PAST_UNROLL
        piece = PAST_UNROLL // 2
        while piece:
            take = (qi & piece) != 0

            def leftover(done=done, piece=piece):
                for u in range(piece):
                    attend(done + u, False)

            pl.when(take)(leftover)
            done = done + jnp.where(take, piece, 0)
            piece //= 2

    bound = ctl_ref[0]
    use_fixed_shift = ctl_ref[1] > 0.5
    pl.when(use_fixed_shift)(lambda: attend_all(attend_fixed_shift))
    pl.when(jnp.logical_not(use_fixed_shift))(lambda: attend_all(attend_running_max))

    for h in range(B_HEADS):
        num = acc_ref[pl.ds(h * VT_ROWS, HEAD_DIM), :]
        den = acc_ref[pl.ds(h * VT_ROWS + HEAD_DIM, 1), :]
        out_ref[pl.ds(h * HEAD_DIM, HEAD_DIM), :] = num / den
    o_ref[...] = out_ref[...].T.astype(BF16)


def _moba(ctl, qt, kblk, vtblk, kmbd):
    b, w, s = qt.shape
    nb = s // MOBA_BLOCK
    stat = pltpu.VMEM((2 * SUBLANES, MOBA_BLOCK), F32)
    return pl.pallas_call(
        _moba_body,
        grid=(b, nb),
        in_specs=[
            pl.BlockSpec(memory_space=pltpu.SMEM),
            pl.BlockSpec((None, w, MOBA_BLOCK), lambda i, j: (i, 0, j)),
            pl.BlockSpec((None, nb, MOBA_BLOCK, w), lambda i, j: (i, 0, 0, 0)),
            pl.BlockSpec((None, nb, B_HEADS * VT_ROWS, MOBA_BLOCK), lambda i, j: (i, 0, 0, 0)),
            pl.BlockSpec((None, B_HEADS * nb, w), lambda i, j: (i, 0, 0)),
        ],
        out_specs=pl.BlockSpec((None, MOBA_BLOCK, w), lambda i, j: (i, j, 0)),
        out_shape=jax.ShapeDtypeStruct((b, s, w), BF16),
        scratch_shapes=[pltpu.VMEM((B_HEADS * nb, MOBA_BLOCK), F32),
                        pltpu.VMEM((B_HEADS, LANES, MOBA_BLOCK), BF16),
                        pltpu.VMEM((B_HEADS, MOBA_BLOCK, MOBA_BLOCK), F32),
                        pltpu.VMEM((B_HEADS, MOBA_BLOCK, MOBA_BLOCK), BF16),
                        stat, stat, stat,
                        pltpu.VMEM((B_HEADS * VT_ROWS, MOBA_BLOCK), F32),
                        pltpu.VMEM((w, MOBA_BLOCK), F32)],
        compiler_params=_params(("arbitrary", "arbitrary"), 56),
        name="moba",
    )(ctl, qt, kblk, vtblk, kmbd)


def _row(v):
    return v.reshape(1, -1).astype(F32)


def _tiled_row(v, reps):
    return jnp.tile(v.astype(F32), reps).reshape(1, -1)


def kernel(x, mem, positions, ffn1_norm_g, ffn1_w_gate, ffn1_w_up, ffn1_w_down, mix_norm_g, mem_norm_g, w_mem_kv, mem_q_norm_g, mem_k_norm_g, w_o, ffn2_norm_g, ffn2_w_gate, ffn2_w_up, ffn2_w_down, a_w_in, a_dw_kernel, a_dw_bias, a_ln_g, a_ln_b, kv_norm_g, w_kv, k_norm_g, b_w_in, b_q_norm_g):
    b, s, d = x.shape
    t = b * s
    nb = s // MOBA_BLOCK
    seg_mem = _seg_mean_matrix(MEM_WIDTH)

    def ffn(h, norm_g, wg, wu, wd, layer, mix=None):
        out = _ffn(h.reshape(t, d), _row(norm_g[layer]), wg, wu, wd, layer, mix)
        return out.reshape(b, s, d)

    def memkv(layer):
        return _memkv(mem, _row(mem_norm_g[layer]), w_mem_kv[layer].astype(BF16),
                      _tiled_row(mem_k_norm_g[layer], MEM_HEADS), seg_mem)

    h = ffn(x, ffn1_norm_g, ffn1_w_gate, ffn1_w_up, ffn1_w_down, 0)
    kt0, v0 = memkv(0)
    dw = jnp.repeat(a_dw_kernel[0].reshape(CONV_WIDTH, CONV_CH), SUBLANES, axis=0)
    dw = dw.reshape(CONV_WIDTH * SUBLANES, CONV_CH // LANES, LANES).transpose(1, 0, 2)
    h = _mixer_a(h, _row(mix_norm_g[0]), a_w_in[0].astype(BF16), dw, _row(a_dw_bias[0]),
                 _row(a_ln_g[0]), _row(a_ln_b[0]), kt0, v0,
                 _tiled_row(mem_q_norm_g[0], MEM_HEADS), seg_mem, w_o[0].astype(BF16))
    h = ffn(h, ffn2_norm_g, ffn2_w_gate, ffn2_w_up, ffn2_w_down, 0)

    inv_freq = 1.0 / (ROPE_THETA ** (jnp.arange(0, ROPE_DIM, 2, dtype=F32) / ROPE_DIM))
    invf = jnp.concatenate([inv_freq, inv_freq]).reshape(ROPE_DIM, 1)
    cos, sin, cst = _rope_tables(positions.reshape(b, 1, s), invf)
    k, vt, km = _shared_kv(h, _row(kv_norm_g), w_kv.astype(BF16),
                           _tiled_row(k_norm_g, B_HEADS), seg_mem, cos, sin)
    kblk = k.reshape(b, nb, MOBA_BLOCK, PRIMARY_WIDTH)
    kmh = km.reshape(b, nb, B_HEADS, HEAD_DIM).transpose(0, 2, 1, 3)
    eye = jnp.eye(B_HEADS, dtype=F32)
    kmbd = (kmh[:, :, :, None, :] * eye[None, :, None, :, None]).reshape(
        b, B_HEADS * nb, PRIMARY_WIDTH).astype(BF16)

    h = ffn(h, ffn1_norm_g, ffn1_w_gate, ffn1_w_up, ffn1_w_down, 1)
    kt1, v1 = memkv(1)
    proj_tm = 512
    q_gain_t = jnp.broadcast_to(jnp.tile(b_q_norm_g[0].astype(F32), B_HEADS)[:, None],
                                (PRIMARY_WIDTH, proj_tm))
    head_of = np.arange(PRIMARY_WIDTH) // HEAD_DIM
    seg_rows = jnp.asarray((np.arange(2 * SUBLANES)[:, None] == head_of[None, :]) / HEAD_DIM, dtype=BF16)
    qt, mem_out = _proj_b(h, _row(mix_norm_g[1]), b_w_in[0].astype(BF16), q_gain_t, seg_rows, cst,
                          kt1, v1, _tiled_row(mem_q_norm_g[1], MEM_HEADS), seg_mem, tm=proj_tm)
    score_bound = (SCORE_BOUND_PER_GAIN * jnp.max(jnp.abs(b_q_norm_g[0]))
                   * jnp.max(jnp.abs(k_norm_g))).astype(F32)
    ctl = jnp.stack([score_bound, (score_bound <= MAX_FIXED_SHIFT).astype(F32)])
    prim = _moba(ctl, qt, kblk, vt, kmbd)
    mix = (prim.reshape(t, PRIMARY_WIDTH), mem_out.reshape(t, MEM_WIDTH), w_o[1].astype(BF16))
    return ffn(h, ffn2_norm_g, ffn2_w_gate, ffn2_w_up, ffn2_w_down, 1, mix)
```

```python
import functools

import numpy as np
import jax
import jax.numpy as jnp
from jax import lax
from jax.experimental import pallas as pl
from jax.experimental.pallas import tpu as pltpu

F32 = jnp.float32
BF16 = jnp.bfloat16

D_MODEL = 1024
HEAD_DIM = 64
MEM_HEADS = 4
MEM_WIDTH = MEM_HEADS * HEAD_DIM
PRIMARY_WIDTH = D_MODEL - MEM_WIDTH
B_HEADS = PRIMARY_WIDTH // HEAD_DIM
CONV_CH = PRIMARY_WIDTH
CONV_WIDTH = 31
MOBA_BLOCK = 256
MOBA_TOPK = 3
ROPE_THETA = 500000.0
ROPE_DIM = HEAD_DIM // 4
EPS = 1e-6
SCALE = HEAD_DIM ** -0.5
LOG2E = float(np.log2(np.e))

LANES = 128
SUBLANES = 8
HALO = 32
CONV_ROWS = 64
FFN_CHUNK = 256
MASKED = -1e30
VT_ROWS = HEAD_DIM + 16
SCORE_BOUND_PER_GAIN = 1.05 * HEAD_DIM * SCALE * LOG2E
MAX_FIXED_SHIFT = 60.0
PAST_UNROLL = 2
MIB = 1024 * 1024


def _params(semantics, vmem_mib):
    return pltpu.CompilerParams(dimension_semantics=semantics,
                                vmem_limit_bytes=vmem_mib * MIB)


def _rms(x, g):
    ms = jnp.mean(x * x, axis=-1, keepdims=True)
    return x * lax.rsqrt(ms + EPS) * g


def _head_rms(x, g, seg_mean):
    ms = jnp.dot((x * x).astype(BF16), seg_mean, preferred_element_type=F32)
    return x * lax.rsqrt(ms + EPS) * g


def _seg_mean_matrix(width):
    idx = np.arange(width) // HEAD_DIM
    return jnp.asarray((idx[:, None] == idx[None, :]).astype(np.float32) / HEAD_DIM, dtype=BF16)


def _rope(x, cos, sin):
    lane = lax.broadcasted_iota(jnp.int32, (1, LANES), 1) % HEAD_DIM
    first_half = lane < (ROPE_DIM // 2)
    outs = []
    for c in range(x.shape[1] // LANES):
        xc = x[:, c * LANES:(c + 1) * LANES]
        partner = jnp.where(first_half,
                            pltpu.roll(xc, LANES - ROPE_DIM // 2, 1),
                            pltpu.roll(xc, ROPE_DIM // 2, 1))
        outs.append(xc * cos + partner * sin)
    return jnp.concatenate(outs, axis=1)


def _mem_attention(qm, kt, v, qg, seg_mean):
    qn = _head_rms(qm, qg, seg_mean) * SCALE
    lane_head = lax.broadcasted_iota(jnp.int32, (1, MEM_WIDTH), 1) // HEAD_DIM
    out = jnp.zeros(qm.shape, F32)
    for h in range(MEM_HEADS):
        qh = jnp.where(lane_head == h, qn, 0.0).astype(BF16)
        s = jnp.dot(qh, kt, preferred_element_type=F32)
        m = jnp.max(s, axis=-1, keepdims=True)
        p = jnp.exp(s - m)
        l = jnp.sum(p, axis=-1, keepdims=True)
        vh = jnp.where(lane_head == h, v, jnp.zeros_like(v))
        out = out + jnp.dot(p.astype(BF16), vh, preferred_element_type=F32) / l
    return out


def _ffn_body(has_mix, *refs):
    if has_mix:
        prim_ref, mem_ref, wo_ref, *refs = refs
    x_ref, g_ref, wg_ref, wu_ref, wd_ref, o_ref, hmid_ref = refs
    x = x_ref[...]
    if has_mix:
        x = (x + jnp.dot(prim_ref[...], wo_ref[0:PRIMARY_WIDTH, :], preferred_element_type=F32)
             + jnp.dot(mem_ref[...], wo_ref[PRIMARY_WIDTH:, :], preferred_element_type=F32))
    xn = _rms(x, g_ref[...]).astype(BF16)
    for c in range(wg_ref.shape[1] // FFN_CHUNK):
        cols = pl.ds(c * FFN_CHUNK, FFN_CHUNK)
        gate = jnp.dot(xn, wg_ref[:, cols], preferred_element_type=F32)
        up = jnp.dot(xn, wu_ref[:, cols], preferred_element_type=F32)
        hmid_ref[:, cols] = (gate * jax.nn.sigmoid(gate) * up).astype(BF16)
    o_ref[...] = x + 0.5 * jnp.dot(hmid_ref[...], wd_ref[...], preferred_element_type=F32)


def _ffn(h2d, g, wg, wu, wd, layer, mix=None, *, tm=512):
    t, d = h2d.shape
    f = wg.shape[2]
    resident = dict(pipeline_mode=pl.Buffered(1))
    mix_args, mix_specs = (), []
    if mix is not None:
        mix_args = mix
        mix_specs = [
            pl.BlockSpec((tm, PRIMARY_WIDTH), lambda i: (i, 0)),
            pl.BlockSpec((tm, MEM_WIDTH), lambda i: (i, 0)),
            pl.BlockSpec((d, d), lambda i: (0, 0), **resident),
        ]
    return pl.pallas_call(
        functools.partial(_ffn_body, mix is not None),
        grid=(t // tm,),
        in_specs=mix_specs + [
            pl.BlockSpec((tm, d), lambda i: (i, 0)),
            pl.BlockSpec((1, d), lambda i: (0, 0)),
            pl.BlockSpec((None, d, f), lambda i: (layer, 0, 0), **resident),
            pl.BlockSpec((None, d, f), lambda i: (layer, 0, 0), **resident),
            pl.BlockSpec((None, f, d), lambda i: (layer, 0, 0), **resident),
        ],
        out_specs=pl.BlockSpec((tm, d), lambda i: (i, 0)),
        out_shape=jax.ShapeDtypeStruct((t, d), F32),
        scratch_shapes=[pltpu.VMEM((tm, f), BF16)],
        compiler_params=_params(("arbitrary",), 56),
        name="ffn_mix" if mix is not None else "ffn",
    )(*mix_args, h2d, g, wg, wu, wd)


def _memkv_body(mem_ref, g_ref, w_ref, kg_ref, seg_ref, kt_ref, v_ref):
    mn = _rms(mem_ref[...], g_ref[...]).astype(BF16)
    kv = jnp.dot(mn, w_ref[...], preferred_element_type=F32)
    k = _head_rms(kv[:, :MEM_WIDTH], kg_ref[...], seg_ref[...])
    kt_ref[...] = k.T.astype(BF16)
    v_ref[...] = kv[:, MEM_WIDTH:].astype(BF16)


def _memkv(mem, g, w, kg, seg):
    b, m, d = mem.shape
    return pl.pallas_call(
        _memkv_body,
        grid=(b,),
        in_specs=[
            pl.BlockSpec((None, m, d), lambda i: (i, 0, 0)),
            pl.BlockSpec((1, d), lambda i: (0, 0)),
            pl.BlockSpec((d, 2 * MEM_WIDTH), lambda i: (0, 0)),
            pl.BlockSpec((1, MEM_WIDTH), lambda i: (0, 0)),
            pl.BlockSpec((MEM_WIDTH, MEM_WIDTH), lambda i: (0, 0)),
        ],
        out_specs=[
            pl.BlockSpec((None, MEM_WIDTH, m), lambda i: (i, 0, 0)),
            pl.BlockSpec((None, m, MEM_WIDTH), lambda i: (i, 0, 0)),
        ],
        out_shape=[jax.ShapeDtypeStruct((b, MEM_WIDTH, m), BF16),
                   jax.ShapeDtypeStruct((b, m, MEM_WIDTH), BF16)],
        compiler_params=_params(("arbitrary",), 32),
        name="memkv",
    )(mem, g, w, kg, seg)


def _mixer_a_body(tm, h_ref, g_ref, win_ref, dw_ref, db_ref, lng_ref, lnb_ref,
                  kt_ref, v_ref, qg_ref, seg_ref, wo_ref, o_ref, buf_ref, conv_ref):
    n_strips = CONV_CH // LANES

    @pl.when(pl.program_id(1) == 0)
    def _():
        buf_ref[0, :, 0:HALO, :] = jnp.zeros((n_strips, HALO, LANES), F32)

    h = h_ref[...]
    hn = _rms(h, g_ref[...]).astype(BF16)
    u = jnp.dot(hn, win_ref[...], preferred_element_type=F32)
    glu = u[:, :CONV_CH] * jax.nn.sigmoid(u[:, CONV_CH:2 * CONV_CH])

    groups = CONV_ROWS // SUBLANES
    for cb in range(n_strips):
        buf_ref[0, cb, HALO:HALO + tm, :] = glu[:, cb * LANES:(cb + 1) * LANES]
        x_strip = buf_ref[0, cb]
        for r in range(1, SUBLANES):
            buf_ref[r, cb] = pltpu.roll(x_strip, r, 0)

        def conv_chunk(c, carry, cb=cb):
            base = pl.multiple_of(c * CONV_ROWS, CONV_ROWS)
            accs = [jnp.zeros((SUBLANES, LANES), F32)] * groups
            for j in range(CONV_WIDTH):
                k = CONV_WIDTH - 1 - j
                start = base + (HALO - SUBLANES * (j // SUBLANES))
                w8 = dw_ref[cb, k * SUBLANES:(k + 1) * SUBLANES, :]
                xs = buf_ref[j % SUBLANES, cb, pl.ds(start, CONV_ROWS), :]
                accs = [accs[i] + w8 * xs[i * SUBLANES:(i + 1) * SUBLANES] for i in range(groups)]
            conv_ref[pl.ds(base, CONV_ROWS), pl.ds(cb * LANES, LANES)] = jnp.concatenate(accs, axis=0)
            return carry

        lax.fori_loop(0, tm // CONV_ROWS, conv_chunk, 0, unroll=2)
    buf_ref[0, :, 0:HALO, :] = buf_ref[0, :, tm:tm + HALO, :]

    c = conv_ref[...] + db_ref[...]
    mu = jnp.mean(c, axis=-1, keepdims=True)
    xc = c - mu
    var = jnp.mean(xc * xc, axis=-1, keepdims=True)
    y = xc * lax.rsqrt(var + EPS) * lng_ref[...] + lnb_ref[...]
    prim = (y * jax.nn.sigmoid(y)).astype(BF16)

    mem = _mem_attention(u[:, 2 * CONV_CH:], kt_ref[...], v_ref[...], qg_ref[...], seg_ref[...])
    o_ref[...] = (h
                  + jnp.dot(prim, wo_ref[0:CONV_CH, :], preferred_element_type=F32)
                  + jnp.dot(mem.astype(BF16), wo_ref[CONV_CH:, :], preferred_element_type=F32))


def _mixer_a(h, g, win, dw, db, lng, lnb, kt, v, qg, seg, wo, *, tm=512):
    b, s, d = h.shape
    m = kt.shape[2]
    const = lambda i, j: (0, 0)
    return pl.pallas_call(
        functools.partial(_mixer_a_body, tm),
        grid=(b, s // tm),
        in_specs=[
            pl.BlockSpec((None, tm, d), lambda i, j: (i, j, 0)),
            pl.BlockSpec((1, d), const),
            pl.BlockSpec(win.shape, const),
            pl.BlockSpec(dw.shape, lambda i, j: (0, 0, 0)),
            pl.BlockSpec((1, CONV_CH), const),
            pl.BlockSpec((1, CONV_CH), const),
            pl.BlockSpec((1, CONV_CH), const),
            pl.BlockSpec((None, MEM_WIDTH, m), lambda i, j: (i, 0, 0)),
            pl.BlockSpec((None, m, MEM_WIDTH), lambda i, j: (i, 0, 0)),
            pl.BlockSpec((1, MEM_WIDTH), const),
            pl.BlockSpec((MEM_WIDTH, MEM_WIDTH), const),
            pl.BlockSpec((d, d), const),
        ],
        out_specs=pl.BlockSpec((None, tm, d), lambda i, j: (i, j, 0)),
        out_shape=jax.ShapeDtypeStruct((b, s, d), F32),
        scratch_shapes=[pltpu.VMEM((SUBLANES, CONV_CH // LANES, tm + HALO, LANES), F32),
                        pltpu.VMEM((tm, CONV_CH), F32)],
        compiler_params=_params(("arbitrary", "arbitrary"), 48),
        name="mixer_a",
    )(h, g, win, dw, db, lng, lnb, kt, v, qg, seg, wo)


def _rope_tab_body(pos_ref, invf_ref, cos_ref, sin_ref, cst_ref):
    tm = pos_ref.shape[1]
    ang = invf_ref[...] * pos_ref[...].astype(F32)
    c = jnp.cos(ang)
    s = jnp.sin(ang)
    half = ROPE_DIM // 2
    cst_ref[...] = jnp.concatenate([c[:half], s[:half]], axis=0)
    row = lax.broadcasted_iota(jnp.int32, ang.shape, 0)
    s = jnp.where(row < ROPE_DIM // 2, -s, s)
    rest = HEAD_DIM - ROPE_DIM
    ones = jnp.ones((rest, tm), F32)
    zeros = jnp.zeros((rest, tm), F32)
    cos_ref[...] = jnp.concatenate([c, ones, c, ones], axis=0).T
    sin_ref[...] = jnp.concatenate([s, zeros, s, zeros], axis=0).T


def _rope_tables(pos3, invf, *, tm=512):
    b, _, s = pos3.shape
    return pl.pallas_call(
        _rope_tab_body,
        grid=(b, s // tm),
        in_specs=[
            pl.BlockSpec((None, 1, tm), lambda i, j: (i, 0, j)),
            pl.BlockSpec((ROPE_DIM, 1), lambda i, j: (0, 0)),
        ],
        out_specs=[pl.BlockSpec((None, tm, LANES), lambda i, j: (i, j, 0)),
                   pl.BlockSpec((None, tm, LANES), lambda i, j: (i, j, 0)),
                   pl.BlockSpec((None, ROPE_DIM, tm), lambda i, j: (i, 0, j))],
        out_shape=[jax.ShapeDtypeStruct((b, s, LANES), F32),
                   jax.ShapeDtypeStruct((b, s, LANES), F32),
                   jax.ShapeDtypeStruct((b, ROPE_DIM, s), F32)],
        compiler_params=_params(("arbitrary", "arbitrary"), 32),
        name="rope_tab",
    )(pos3, invf)


def _shared_kv_body(tm, h_ref, g_ref, w_ref, kg_ref, seg_ref, cos_ref, sin_ref,
                    k_ref, vt_ref, km_ref):
    hn = _rms(h_ref[...], g_ref[...]).astype(BF16)
    k = jnp.dot(hn, w_ref[:, 0:PRIMARY_WIDTH], preferred_element_type=F32)
    v = jnp.dot(hn, w_ref[:, PRIMARY_WIDTH:], preferred_element_type=F32)
    for c in range(PRIMARY_WIDTH // FFN_CHUNK):
        cols = pl.ds(c * FFN_CHUNK, FFN_CHUNK)
        kc = k[:, c * FFN_CHUNK:(c + 1) * FFN_CHUNK]
        kr = _rope(_head_rms(kc, kg_ref[:, cols], seg_ref[...]), cos_ref[...], sin_ref[...])
        kb = kr.astype(BF16)
        for blk in range(tm // MOBA_BLOCK):
            rows = slice(blk * MOBA_BLOCK, (blk + 1) * MOBA_BLOCK)
            km_ref[blk, :, cols] = jnp.mean(kr[rows], axis=0, keepdims=True)
            for pp in range(FFN_CHUNK // LANES):
                k_ref[blk, c * (FFN_CHUNK // LANES) + pp] = kb[rows, pp * LANES:(pp + 1) * LANES]
    ones = jnp.ones((VT_ROWS - HEAD_DIM, MOBA_BLOCK), BF16)
    for blk in range(tm // MOBA_BLOCK):
        rows = slice(blk * MOBA_BLOCK, (blk + 1) * MOBA_BLOCK)
        vt = v[rows].T.astype(BF16)
        for h in range(B_HEADS):
            vt_ref[blk, h * VT_ROWS:h * VT_ROWS + HEAD_DIM, :] = vt[h * HEAD_DIM:(h + 1) * HEAD_DIM]
            vt_ref[blk, h * VT_ROWS + HEAD_DIM:(h + 1) * VT_ROWS, :] = ones


def _shared_kv(h, g, w, kg, seg, cos, sin, *, tm=512):
    b, s, d = h.shape
    nb = s // MOBA_BLOCK
    bpt = tm // MOBA_BLOCK
    const = lambda i, j: (0, 0)
    return pl.pallas_call(
        functools.partial(_shared_kv_body, tm),
        grid=(b, s // tm),
        in_specs=[
            pl.BlockSpec((None, tm, d), lambda i, j: (i, j, 0)),
            pl.BlockSpec((1, d), const),
            pl.BlockSpec(w.shape, const),
            pl.BlockSpec((1, PRIMARY_WIDTH), const),
            pl.BlockSpec((FFN_CHUNK, FFN_CHUNK), const),
            pl.BlockSpec((None, tm, LANES), lambda i, j: (i, j, 0)),
            pl.BlockSpec((None, tm, LANES), lambda i, j: (i, j, 0)),
        ],
        out_specs=[
            pl.BlockSpec((None, bpt, PRIMARY_WIDTH // LANES, MOBA_BLOCK, LANES),
                         lambda i, j: (i, j, 0, 0, 0)),
            pl.BlockSpec((None, bpt, B_HEADS * VT_ROWS, MOBA_BLOCK), lambda i, j: (i, j, 0, 0)),
            pl.BlockSpec((None, bpt, 1, PRIMARY_WIDTH), lambda i, j: (i, j, 0, 0)),
        ],
        out_shape=[jax.ShapeDtypeStruct((b, nb, PRIMARY_WIDTH // LANES, MOBA_BLOCK, LANES), BF16),
                   jax.ShapeDtypeStruct((b, nb, B_HEADS * VT_ROWS, MOBA_BLOCK), BF16),
                   jax.ShapeDtypeStruct((b, nb, 1, PRIMARY_WIDTH), F32)],
        compiler_params=_params(("arbitrary", "arbitrary"), 48),
        name="shared_kv",
    )(h, g, w, kg, seg, cos, sin)


def _proj_b_body(h_ref, g_ref, win_ref, qg_ref, seg_ref, cst_ref,
                 kt_ref, v_ref, mqg_ref, mseg_ref, qt_ref, mem_ref):
    hn = _rms(h_ref[...], g_ref[...]).astype(BF16)
    u = jnp.dot(hn, win_ref[...], preferred_element_type=F32)
    qt = u[:, :PRIMARY_WIDTH].T
    ms = jnp.dot(seg_ref[...], (qt * qt).astype(BF16), preferred_element_type=F32)
    inv = lax.rsqrt(ms + EPS) * (SCALE * LOG2E)
    half = ROPE_DIM // 2
    cos = cst_ref[0:half, :]
    sin = cst_ref[half:ROPE_DIM, :]
    for h in range(B_HEADS):
        rows = slice(h * HEAD_DIM, (h + 1) * HEAD_DIM)
        qh = qt[rows] * inv[h:h + 1, :] * qg_ref[rows, :]
        x1, x2 = qh[:half], qh[half:ROPE_DIM]
        rot = jnp.concatenate([x1 * cos - x2 * sin, x2 * cos + x1 * sin], axis=0)
        qt_ref[h * HEAD_DIM:h * HEAD_DIM + ROPE_DIM, :] = rot.astype(BF16)
        qt_ref[h * HEAD_DIM + ROPE_DIM:(h + 1) * HEAD_DIM, :] = qh[ROPE_DIM:].astype(BF16)
    mem = _mem_attention(u[:, PRIMARY_WIDTH:], kt_ref[...], v_ref[...], mqg_ref[...], mseg_ref[...])
    mem_ref[...] = mem.astype(BF16)


def _proj_b(h, g, win, qg, seg, cst, kt, v, mqg, mseg, *, tm=512):
    b, s, d = h.shape
    m = kt.shape[2]
    const = lambda i, j: (0, 0)
    return pl.pallas_call(
        _proj_b_body,
        grid=(b, s // tm),
        in_specs=[
            pl.BlockSpec((None, tm, d), lambda i, j: (i, j, 0)),
            pl.BlockSpec((1, d), const),
            pl.BlockSpec(win.shape, const),
            pl.BlockSpec((PRIMARY_WIDTH, tm), const),
            pl.BlockSpec((2 * SUBLANES, PRIMARY_WIDTH), const),
            pl.BlockSpec((None, ROPE_DIM, tm), lambda i, j: (i, 0, j)),
            pl.BlockSpec((None, MEM_WIDTH, m), lambda i, j: (i, 0, 0)),
            pl.BlockSpec((None, m, MEM_WIDTH), lambda i, j: (i, 0, 0)),
            pl.BlockSpec((1, MEM_WIDTH), const),
            pl.BlockSpec((MEM_WIDTH, MEM_WIDTH), const),
        ],
        out_specs=[pl.BlockSpec((None, PRIMARY_WIDTH, tm), lambda i, j: (i, 0, j)),
                   pl.BlockSpec((None, tm, MEM_WIDTH), lambda i, j: (i, j, 0))],
        out_shape=[jax.ShapeDtypeStruct((b, PRIMARY_WIDTH, s), BF16),
                   jax.ShapeDtypeStruct((b, s, MEM_WIDTH), BF16)],
        compiler_params=_params(("arbitrary", "arbitrary"), 48),
        name="proj_b",
    )(h, g, win, qg, seg, cst, kt, v, mqg, mseg)


def _moba_body(ctl_ref, qt_ref, k_ref, vt_ref, km_ref, o_ref,
               bias_ref, qh_ref, s_ref, p_ref, m_ref, shift_ref, alpha_ref, acc_ref, out_ref):
    qi = pl.program_id(1)
    nb = km_ref.shape[0] // B_HEADS
    tq = qt_ref.shape[1]
    qt = qt_ref[...]

    def select_blocks():
        gate = jnp.dot(km_ref[...], qt, preferred_element_type=F32)
        blk = lax.broadcasted_iota(jnp.int32, (nb, tq), 0)
        for h in range(B_HEADS):
            g = gate[h * nb:(h + 1) * nb, :]
            rank = jnp.zeros((nb, tq), F32)
            for m in range(nb):
                gm = g[m:m + 1, :]
                valid = (m < qi).astype(F32)
                rank = rank + jnp.where(m < blk, jnp.where(gm >= g, valid, 0.0),
                                        jnp.where(gm > g, valid, 0.0))
            bias_ref[h * nb:(h + 1) * nb, :] = jnp.where(rank < MOBA_TOPK, 0.0, MASKED)

    pair_row = lax.broadcasted_iota(jnp.int32, (LANES, tq), 0)
    for h in range(B_HEADS):
        pair = h // 2
        in_head = (pair_row // HEAD_DIM) == (h % 2)
        qh_ref[h] = jnp.where(in_head, qt[pair * LANES:(pair + 1) * LANES, :],
                              jnp.zeros((LANES, tq), BF16))

    key_pos = lax.broadcasted_iota(jnp.int32, (MOBA_BLOCK, tq), 0)
    q_pos = lax.broadcasted_iota(jnp.int32, (MOBA_BLOCK, tq), 1)
    causal = key_pos <= q_pos

    def scores(kj, h, own):
        s = jnp.dot(k_ref[kj, h // 2], qh_ref[h], preferred_element_type=F32)
        return jnp.where(causal, s, MASKED) if own else s

    def head_rows(h):
        return pl.ds(h * VT_ROWS, VT_ROWS)

    def attend_fixed_shift(kj, own):
        for h in range(B_HEADS):
            shift = bound if own else bound - bias_ref[pl.ds(h * nb + kj, 1), :]
            p_ref[h] = jnp.exp2(scores(kj, h, own) - shift).astype(BF16)
        for h in range(B_HEADS):
            pv = jnp.dot(vt_ref[kj, head_rows(h), :], p_ref[h], preferred_element_type=F32)
            if own:
                acc_ref[head_rows(h), :] = pv
            else:
                acc_ref[head_rows(h), :] += pv

    def attend_running_max(kj, own):
        for h in range(B_HEADS):
            s_ref[h] = scores(kj, h, own)
        for h in range(B_HEADS):
            blk_max = jnp.max(s_ref[h], axis=0, keepdims=True)
            if own:
                m_ref[h:h + 1, :] = blk_max
                shift_ref[h:h + 1, :] = blk_max
            else:
                bias = bias_ref[pl.ds(h * nb + kj, 1), :]
                m_old = m_ref[h:h + 1, :]
                m_new = jnp.maximum(m_old, blk_max + bias)
                shift_ref[h:h + 1, :] = m_new - bias
                alpha_ref[h:h + 1, :] = jnp.exp2(m_old - m_new)
                m_ref[h:h + 1, :] = m_new
        for h in range(B_HEADS):
            p_ref[h] = jnp.exp2(s_ref[h] - shift_ref[h:h + 1, :]).astype(BF16)
        for h in range(B_HEADS):
            pv = jnp.dot(vt_ref[kj, head_rows(h), :], p_ref[h], preferred_element_type=F32)
            if own:
                acc_ref[head_rows(h), :] = pv
            else:
                acc_ref[head_rows(h), :] = alpha_ref[h:h + 1, :] * acc_ref[head_rows(h), :] + pv

    def attend_all(attend):
        attend(qi, True)
        select_blocks()

        def past_blocks(i, carry):
            for u in range(PAST_UNROLL):
                attend(PAST_UNROLL * i + u, False)
            return carry

        groups = lax.shift_right_logical(qi, PAST_UNROLL.bit_length() - 1)
        lax.fori_loop(0, groups, past_blocks, 0)
        done = groups * PAST_UNROLL
        piece = PAST_UNROLL // 2
        while piece:
            take = (qi & piece) != 0

            def leftover(done=done, piece=piece):
                for u in range(piece):
                    attend(done + u, False)

            pl.when(take)(leftover)
            done = done + jnp.where(take, piece, 0)
            piece //= 2

    bound = ctl_ref[0]
    use_fixed_shift = ctl_ref[1] > 0.5
    pl.when(use_fixed_shift)(lambda: attend_all(attend_fixed_shift))
    pl.when(jnp.logical_not(use_fixed_shift))(lambda: attend_all(attend_running_max))

    for h in range(B_HEADS):
        num = acc_ref[pl.ds(h * VT_ROWS, HEAD_DIM), :]
        den = acc_ref[pl.ds(h * VT_ROWS + HEAD_DIM, 1), :]
        out_ref[pl.ds(h * HEAD_DIM, HEAD_DIM), :] = num / den
    o_ref[...] = out_ref[...].T.astype(BF16)


def _moba(ctl, qt, kblk, vtblk, kmbd):
    b, w, s = qt.shape
    nb = s // MOBA_BLOCK
    stat = pltpu.VMEM((2 * SUBLANES, MOBA_BLOCK), F32)
    return pl.pallas_call(
        _moba_body,
        grid=(b, nb),
        in_specs=[
            pl.BlockSpec(memory_space=pltpu.SMEM),
            pl.BlockSpec((None, w, MOBA_BLOCK), lambda i, j: (i, 0, j)),
            pl.BlockSpec((None, nb, w // LANES, MOBA_BLOCK, LANES), lambda i, j: (i, 0, 0, 0, 0)),
            pl.BlockSpec((None, nb, B_HEADS * VT_ROWS, MOBA_BLOCK), lambda i, j: (i, 0, 0, 0)),
            pl.BlockSpec((None, B_HEADS * nb, w), lambda i, j: (i, 0, 0)),
        ],
        out_specs=pl.BlockSpec((None, MOBA_BLOCK, w), lambda i, j: (i, j, 0)),
        out_shape=jax.ShapeDtypeStruct((b, s, w), BF16),
        scratch_shapes=[pltpu.VMEM((B_HEADS * nb, MOBA_BLOCK), F32),
                        pltpu.VMEM((B_HEADS, LANES, MOBA_BLOCK), BF16),
                        pltpu.VMEM((B_HEADS, MOBA_BLOCK, MOBA_BLOCK), F32),
                        pltpu.VMEM((B_HEADS, MOBA_BLOCK, MOBA_BLOCK), BF16),
                        stat, stat, stat,
                        pltpu.VMEM((B_HEADS * VT_ROWS, MOBA_BLOCK), F32),
                        pltpu.VMEM((w, MOBA_BLOCK), F32)],
        compiler_params=_params(("arbitrary", "arbitrary"), 56),
        name="moba",
    )(ctl, qt, kblk, vtblk, kmbd)


def _row(v):
    return v.reshape(1, -1).astype(F32)


def _tiled_row(v, reps):
    return jnp.tile(v.astype(F32), reps).reshape(1, -1)


def kernel(x, mem, positions, ffn1_norm_g, ffn1_w_gate, ffn1_w_up, ffn1_w_down, mix_norm_g, mem_norm_g, w_mem_kv, mem_q_norm_g, mem_k_norm_g, w_o, ffn2_norm_g, ffn2_w_gate, ffn2_w_up, ffn2_w_down, a_w_in, a_dw_kernel, a_dw_bias, a_ln_g, a_ln_b, kv_norm_g, w_kv, k_norm_g, b_w_in, b_q_norm_g):
    b, s, d = x.shape
    t = b * s
    nb = s // MOBA_BLOCK
    seg_mem = _seg_mean_matrix(MEM_WIDTH)

    def ffn(h, norm_g, wg, wu, wd, layer, mix=None):
        out = _ffn(h.reshape(t, d), _row(norm_g[layer]), wg, wu, wd, layer, mix)
        return out.reshape(b, s, d)

    def memkv(layer):
        return _memkv(mem, _row(mem_norm_g[layer]), w_mem_kv[layer].astype(BF16),
                      _tiled_row(mem_k_norm_g[layer], MEM_HEADS), seg_mem)

    h = ffn(x, ffn1_norm_g, ffn1_w_gate, ffn1_w_up, ffn1_w_down, 0)
    kt0, v0 = memkv(0)
    dw = jnp.repeat(a_dw_kernel[0].reshape(CONV_WIDTH, CONV_CH), SUBLANES, axis=0)
    dw = dw.reshape(CONV_WIDTH * SUBLANES, CONV_CH // LANES, LANES).transpose(1, 0, 2)
    h = _mixer_a(h, _row(mix_norm_g[0]), a_w_in[0].astype(BF16), dw, _row(a_dw_bias[0]),
                 _row(a_ln_g[0]), _row(a_ln_b[0]), kt0, v0,
                 _tiled_row(mem_q_norm_g[0], MEM_HEADS), seg_mem, w_o[0].astype(BF16))
    h = ffn(h, ffn2_norm_g, ffn2_w_gate, ffn2_w_up, ffn2_w_down, 0)

    inv_freq = 1.0 / (ROPE_THETA ** (jnp.arange(0, ROPE_DIM, 2, dtype=F32) / ROPE_DIM))
    invf = jnp.concatenate([inv_freq, inv_freq]).reshape(ROPE_DIM, 1)
    cos, sin, cst = _rope_tables(positions.reshape(b, 1, s), invf)
    k, vt, km = _shared_kv(h, _row(kv_norm_g), w_kv.astype(BF16),
                           _tiled_row(k_norm_g, B_HEADS), seg_mem, cos, sin)
    kmh = km.reshape(b, nb, B_HEADS, HEAD_DIM).transpose(0, 2, 1, 3)
    eye = jnp.eye(B_HEADS, dtype=F32)
    kmbd = (kmh[:, :, :, None, :] * eye[None, :, None, :, None]).reshape(
        b, B_HEADS * nb, PRIMARY_WIDTH).astype(BF16)

    h = ffn(h, ffn1_norm_g, ffn1_w_gate, ffn1_w_up, ffn1_w_down, 1)
    kt1, v1 = memkv(1)
    proj_tm = 512
    q_gain_t = jnp.broadcast_to(jnp.tile(b_q_norm_g[0].astype(F32), B_HEADS)[:, None],
                                (PRIMARY_WIDTH, proj_tm))
    head_of = np.arange(PRIMARY_WIDTH) // HEAD_DIM
    seg_rows = jnp.asarray((np.arange(2 * SUBLANES)[:, None] == head_of[None, :]) / HEAD_DIM, dtype=BF16)
    qt, mem_out = _proj_b(h, _row(mix_norm_g[1]), b_w_in[0].astype(BF16), q_gain_t, seg_rows, cst,
                          kt1, v1, _tiled_row(mem_q_norm_g[1], MEM_HEADS), seg_mem, tm=proj_tm)
    score_bound = (SCORE_BOUND_PER_GAIN * jnp.max(jnp.abs(b_q_norm_g[0]))
                   * jnp.max(jnp.abs(k_norm_g))).astype(F32)
    ctl = jnp.stack([score_bound, (score_bound <= MAX_FIXED_SHIFT).astype(F32)])
    prim = _moba(ctl, qt, k, vt, kmbd)
    mix = (prim.reshape(t, PRIMARY_WIDTH), mem_out.reshape(t, MEM_WIDTH), w_o[1].astype(BF16))
    return ffn(h, ffn2_norm_g, ffn2_w_gate, ffn2_w_up, ffn2_w_down, 1, mix)
```

```python
import functools

import numpy as np
import jax
import jax.numpy as jnp
from jax import lax
from jax.experimental import pallas as pl
from jax.experimental.pallas import tpu as pltpu

F32 = jnp.float32
BF16 = jnp.bfloat16

D_MODEL = 1024
HEAD_DIM = 64
MEM_HEADS = 4
MEM_WIDTH = MEM_HEADS * HEAD_DIM
PRIMARY_WIDTH = D_MODEL - MEM_WIDTH
B_HEADS = PRIMARY_WIDTH // HEAD_DIM
CONV_CH = PRIMARY_WIDTH
CONV_WIDTH = 31
MOBA_BLOCK = 256
MOBA_TOPK = 3
ROPE_THETA = 500000.0
ROPE_DIM = HEAD_DIM // 4
EPS = 1e-6
SCALE = HEAD_DIM ** -0.5
LOG2E = float(np.log2(np.e))

LANES = 128
SUBLANES = 8
HALO = 32
CONV_ROWS = 64
FFN_CHUNK = 256
MASKED = -1e30
VT_ROWS = HEAD_DIM + 16
SCORE_BOUND_PER_GAIN = 1.05 * HEAD_DIM * SCALE * LOG2E
MAX_FIXED_SHIFT = 60.0
PAST_UNROLL = 2
MIB = 1024 * 1024


def _params(semantics, vmem_mib):
    return pltpu.CompilerParams(dimension_semantics=semantics,
                                vmem_limit_bytes=vmem_mib * MIB)


def _rms(x, g):
    ms = jnp.mean(x * x, axis=-1, keepdims=True)
    return x * lax.rsqrt(ms + EPS) * g


def _head_rms(x, g, seg_mean):
    ms = jnp.dot((x * x).astype(BF16), seg_mean, preferred_element_type=F32)
    return x * lax.rsqrt(ms + EPS) * g


def _seg_mean_matrix(width):
    idx = np.arange(width) // HEAD_DIM
    return jnp.asarray((idx[:, None] == idx[None, :]).astype(np.float32) / HEAD_DIM, dtype=BF16)


def _rope(x, cos, sin):
    lane = lax.broadcasted_iota(jnp.int32, (1, LANES), 1) % HEAD_DIM
    first_half = lane < (ROPE_DIM // 2)
    outs = []
    for c in range(x.shape[1] // LANES):
        xc = x[:, c * LANES:(c + 1) * LANES]
        partner = jnp.where(first_half,
                            pltpu.roll(xc, LANES - ROPE_DIM // 2, 1),
                            pltpu.roll(xc, ROPE_DIM // 2, 1))
        outs.append(xc * cos + partner * sin)
    return jnp.concatenate(outs, axis=1)


def _mem_attention(qm, kt, v, qg, seg_mean):
    qn = _head_rms(qm, qg, seg_mean) * SCALE
    lane_head = lax.broadcasted_iota(jnp.int32, (1, MEM_WIDTH), 1) // HEAD_DIM
    out = jnp.zeros(qm.shape, F32)
    for h in range(MEM_HEADS):
        qh = jnp.where(lane_head == h, qn, 0.0).astype(BF16)
        s = jnp.dot(qh, kt, preferred_element_type=F32)
        m = jnp.max(s, axis=-1, keepdims=True)
        p = jnp.exp(s - m)
        l = jnp.sum(p, axis=-1, keepdims=True)
        vh = jnp.where(lane_head == h, v, jnp.zeros_like(v))
        out = out + jnp.dot(p.astype(BF16), vh, preferred_element_type=F32) / l
    return out


def _ffn_body(has_mix, *refs):
    if has_mix:
        prim_ref, mem_ref, wo_ref, *refs = refs
    x_ref, g_ref, wg_ref, wu_ref, wd_ref, o_ref, hmid_ref = refs
    x = x_ref[...]
    if has_mix:
        x = (x + jnp.dot(prim_ref[...], wo_ref[0:PRIMARY_WIDTH, :], preferred_element_type=F32)
             + jnp.dot(mem_ref[...], wo_ref[PRIMARY_WIDTH:, :], preferred_element_type=F32))
    xn = _rms(x, g_ref[...]).astype(BF16)
    for c in range(wg_ref.shape[1] // FFN_CHUNK):
        cols = pl.ds(c * FFN_CHUNK, FFN_CHUNK)
        gate = jnp.dot(xn, wg_ref[:, cols], preferred_element_type=F32)
        up = jnp.dot(xn, wu_ref[:, cols], preferred_element_type=F32)
        hmid_ref[:, cols] = (gate * jax.nn.sigmoid(gate) * up).astype(BF16)
    o_ref[...] = x + 0.5 * jnp.dot(hmid_ref[...], wd_ref[...], preferred_element_type=F32)


def _ffn(h2d, g, wg, wu, wd, layer, mix=None, *, tm=512):
    t, d = h2d.shape
    f = wg.shape[2]
    resident = dict(pipeline_mode=pl.Buffered(1))
    mix_args, mix_specs = (), []
    if mix is not None:
        mix_args = mix
        mix_specs = [
            pl.BlockSpec((tm, PRIMARY_WIDTH), lambda i: (i, 0)),
            pl.BlockSpec((tm, MEM_WIDTH), lambda i: (i, 0)),
            pl.BlockSpec((d, d), lambda i: (0, 0), **resident),
        ]
    return pl.pallas_call(
        functools.partial(_ffn_body, mix is not None),
        grid=(t // tm,),
        in_specs=mix_specs + [
            pl.BlockSpec((tm, d), lambda i: (i, 0)),
            pl.BlockSpec((1, d), lambda i: (0, 0)),
            pl.BlockSpec((None, d, f), lambda i: (layer, 0, 0), **resident),
            pl.BlockSpec((None, d, f), lambda i: (layer, 0, 0), **resident),
            pl.BlockSpec((None, f, d), lambda i: (layer, 0, 0), **resident),
        ],
        out_specs=pl.BlockSpec((tm, d), lambda i: (i, 0)),
        out_shape=jax.ShapeDtypeStruct((t, d), F32),
        scratch_shapes=[pltpu.VMEM((tm, f), BF16)],
        compiler_params=_params(("arbitrary",), 56),
        name="ffn_mix" if mix is not None else "ffn",
    )(*mix_args, h2d, g, wg, wu, wd)


def _memkv_body(mem_ref, g_ref, w_ref, kg_ref, seg_ref, kt_ref, v_ref):
    mn = _rms(mem_ref[...], g_ref[...]).astype(BF16)
    kv = jnp.dot(mn, w_ref[...], preferred_element_type=F32)
    k = _head_rms(kv[:, :MEM_WIDTH], kg_ref[...], seg_ref[...])
    kt_ref[...] = k.T.astype(BF16)
    v_ref[...] = kv[:, MEM_WIDTH:].astype(BF16)


def _memkv(mem, g, w, kg, seg):
    b, m, d = mem.shape
    return pl.pallas_call(
        _memkv_body,
        grid=(b,),
        in_specs=[
            pl.BlockSpec((None, m, d), lambda i: (i, 0, 0)),
            pl.BlockSpec((1, d), lambda i: (0, 0)),
            pl.BlockSpec((d, 2 * MEM_WIDTH), lambda i: (0, 0)),
            pl.BlockSpec((1, MEM_WIDTH), lambda i: (0, 0)),
            pl.BlockSpec((MEM_WIDTH, MEM_WIDTH), lambda i: (0, 0)),
        ],
        out_specs=[
            pl.BlockSpec((None, MEM_WIDTH, m), lambda i: (i, 0, 0)),
            pl.BlockSpec((None, m, MEM_WIDTH), lambda i: (i, 0, 0)),
        ],
        out_shape=[jax.ShapeDtypeStruct((b, MEM_WIDTH, m), BF16),
                   jax.ShapeDtypeStruct((b, m, MEM_WIDTH), BF16)],
        compiler_params=_params(("arbitrary",), 32),
        name="memkv",
    )(mem, g, w, kg, seg)


def _mixer_a_body(tm, h_ref, g_ref, win_ref, dw_ref, db_ref, lng_ref, lnb_ref,
                  kt_ref, v_ref, qg_ref, seg_ref, wo_ref, o_ref, buf_ref, conv_ref):
    n_strips = CONV_CH // LANES

    @pl.when(pl.program_id(1) == 0)
    def _():
        buf_ref[0, :, 0:HALO, :] = jnp.zeros((n_strips, HALO, LANES), F32)

    h = h_ref[...]
    hn = _rms(h, g_ref[...]).astype(BF16)
    u = jnp.dot(hn, win_ref[...], preferred_element_type=F32)
    glu = u[:, :CONV_CH] * jax.nn.sigmoid(u[:, CONV_CH:2 * CONV_CH])

    groups = CONV_ROWS // SUBLANES
    for cb in range(n_strips):
        buf_ref[0, cb, HALO:HALO + tm, :] = glu[:, cb * LANES:(cb + 1) * LANES]
        x_strip = buf_ref[0, cb]
        for r in range(1, SUBLANES):
            buf_ref[r, cb] = pltpu.roll(x_strip, r, 0)

        def conv_chunk(c, carry, cb=cb):
            base = pl.multiple_of(c * CONV_ROWS, CONV_ROWS)
            accs = [jnp.zeros((SUBLANES, LANES), F32)] * groups
            for j in range(CONV_WIDTH):
                k = CONV_WIDTH - 1 - j
                start = base + (HALO - SUBLANES * (j // SUBLANES))
                w8 = dw_ref[cb, k * SUBLANES:(k + 1) * SUBLANES, :]
                xs = buf_ref[j % SUBLANES, cb, pl.ds(start, CONV_ROWS), :]
                accs = [accs[i] + w8 * xs[i * SUBLANES:(i + 1) * SUBLANES] for i in range(groups)]
            conv_ref[pl.ds(base, CONV_ROWS), pl.ds(cb * LANES, LANES)] = jnp.concatenate(accs, axis=0)
            return carry

        lax.fori_loop(0, tm // CONV_ROWS, conv_chunk, 0, unroll=2)
    buf_ref[0, :, 0:HALO, :] = buf_ref[0, :, tm:tm + HALO, :]

    c = conv_ref[...] + db_ref[...]
    mu = jnp.mean(c, axis=-1, keepdims=True)
    xc = c - mu
    var = jnp.mean(xc * xc, axis=-1, keepdims=True)
    y = xc * lax.rsqrt(var + EPS) * lng_ref[...] + lnb_ref[...]
    prim = (y * jax.nn.sigmoid(y)).astype(BF16)

    mem = _mem_attention(u[:, 2 * CONV_CH:], kt_ref[...], v_ref[...], qg_ref[...], seg_ref[...])
    o_ref[...] = (h
                  + jnp.dot(prim, wo_ref[0:CONV_CH, :], preferred_element_type=F32)
                  + jnp.dot(mem.astype(BF16), wo_ref[CONV_CH:, :], preferred_element_type=F32))


def _mixer_a(h, g, win, dw, db, lng, lnb, kt, v, qg, seg, wo, *, tm=512):
    b, s, d = h.shape
    m = kt.shape[2]
    const = lambda i, j: (0, 0)
    return pl.pallas_call(
        functools.partial(_mixer_a_body, tm),
        grid=(b, s // tm),
        in_specs=[
            pl.BlockSpec((None, tm, d), lambda i, j: (i, j, 0)),
            pl.BlockSpec((1, d), const),
            pl.BlockSpec(win.shape, const),
            pl.BlockSpec(dw.shape, lambda i, j: (0, 0, 0)),
            pl.BlockSpec((1, CONV_CH), const),
            pl.BlockSpec((1, CONV_CH), const),
            pl.BlockSpec((1, CONV_CH), const),
            pl.BlockSpec((None, MEM_WIDTH, m), lambda i, j: (i, 0, 0)),
            pl.BlockSpec((None, m, MEM_WIDTH), lambda i, j: (i, 0, 0)),
            pl.BlockSpec((1, MEM_WIDTH), const),
            pl.BlockSpec((MEM_WIDTH, MEM_WIDTH), const),
            pl.BlockSpec((d, d), const),
        ],
        out_specs=pl.BlockSpec((None, tm, d), lambda i, j: (i, j, 0)),
        out_shape=jax.ShapeDtypeStruct((b, s, d), F32),
        scratch_shapes=[pltpu.VMEM((SUBLANES, CONV_CH // LANES, tm + HALO, LANES), F32),
                        pltpu.VMEM((tm, CONV_CH), F32)],
        compiler_params=_params(("arbitrary", "arbitrary"), 48),
        name="mixer_a",
    )(h, g, win, dw, db, lng, lnb, kt, v, qg, seg, wo)


def _rope_tab_body(pos_ref, invf_ref, cos_ref, sin_ref, cst_ref):
    tm = pos_ref.shape[1]
    ang = invf_ref[...] * pos_ref[...].astype(F32)
    c = jnp.cos(ang)
    s = jnp.sin(ang)
    half = ROPE_DIM // 2
    cst_ref[...] = jnp.concatenate([c[:half], s[:half]], axis=0)
    row = lax.broadcasted_iota(jnp.int32, ang.shape, 0)
    s = jnp.where(row < ROPE_DIM // 2, -s, s)
    rest = HEAD_DIM - ROPE_DIM
    ones = jnp.ones((rest, tm), F32)
    zeros = jnp.zeros((rest, tm), F32)
    cos_ref[...] = jnp.concatenate([c, ones, c, ones], axis=0).T
    sin_ref[...] = jnp.concatenate([s, zeros, s, zeros], axis=0).T


def _rope_tables(pos3, invf, *, tm=512):
    b, _, s = pos3.shape
    return pl.pallas_call(
        _rope_tab_body,
        grid=(b, s // tm),
        in_specs=[
            pl.BlockSpec((None, 1, tm), lambda i, j: (i, 0, j)),
            pl.BlockSpec((ROPE_DIM, 1), lambda i, j: (0, 0)),
        ],
        out_specs=[pl.BlockSpec((None, tm, LANES), lambda i, j: (i, j, 0)),
                   pl.BlockSpec((None, tm, LANES), lambda i, j: (i, j, 0)),
                   pl.BlockSpec((None, ROPE_DIM, tm), lambda i, j: (i, 0, j))],
        out_shape=[jax.ShapeDtypeStruct((b, s, LANES), F32),
                   jax.ShapeDtypeStruct((b, s, LANES), F32),
                   jax.ShapeDtypeStruct((b, ROPE_DIM, s), F32)],
        compiler_params=_params(("arbitrary", "arbitrary"), 32),
        name="rope_tab",
    )(pos3, invf)


def _shared_kv_body(tm, h_ref, g_ref, w_ref, kg_ref, seg_ref, cos_ref, sin_ref,
                    k_ref, vt_ref, km_ref):
    hn = _rms(h_ref[...], g_ref[...]).astype(BF16)
    k = jnp.dot(hn, w_ref[:, 0:PRIMARY_WIDTH], preferred_element_type=F32)
    v = jnp.dot(hn, w_ref[:, PRIMARY_WIDTH:], preferred_element_type=F32)
    for c in range(PRIMARY_WIDTH // FFN_CHUNK):
        cols = pl.ds(c * FFN_CHUNK, FFN_CHUNK)
        kc = k[:, c * FFN_CHUNK:(c + 1) * FFN_CHUNK]
        kr = _rope(_head_rms(kc, kg_ref[:, cols], seg_ref[...]), cos_ref[...], sin_ref[...])
        kb = kr.astype(BF16)
        for blk in range(tm // MOBA_BLOCK):
            rows = slice(blk * MOBA_BLOCK, (blk + 1) * MOBA_BLOCK)
            km_ref[blk, :, cols] = jnp.mean(kr[rows], axis=0, keepdims=True)
            for pp in range(FFN_CHUNK // LANES):
                k_ref[blk, c * (FFN_CHUNK // LANES) + pp] = kb[rows, pp * LANES:(pp + 1) * LANES]
    ones = jnp.ones((VT_ROWS - HEAD_DIM, MOBA_BLOCK), BF16)
    for blk in range(tm // MOBA_BLOCK):
        rows = slice(blk * MOBA_BLOCK, (blk + 1) * MOBA_BLOCK)
        vt = v[rows].T.astype(BF16)
        for h in range(B_HEADS):
            vt_ref[blk, h * VT_ROWS:h * VT_ROWS + HEAD_DIM, :] = vt[h * HEAD_DIM:(h + 1) * HEAD_DIM]
            vt_ref[blk, h * VT_ROWS + HEAD_DIM:(h + 1) * VT_ROWS, :] = ones


def _shared_kv(h, g, w, kg, seg, cos, sin, *, tm=1024):
    b, s, d = h.shape
    nb = s // MOBA_BLOCK
    bpt = tm // MOBA_BLOCK
    const = lambda i, j: (0, 0)
    return pl.pallas_call(
        functools.partial(_shared_kv_body, tm),
        grid=(b, s // tm),
        in_specs=[
            pl.BlockSpec((None, tm, d), lambda i, j: (i, j, 0)),
            pl.BlockSpec((1, d), const),
            pl.BlockSpec(w.shape, const),
            pl.BlockSpec((1, PRIMARY_WIDTH), const),
            pl.BlockSpec((FFN_CHUNK, FFN_CHUNK), const),
            pl.BlockSpec((None, tm, LANES), lambda i, j: (i, j, 0)),
            pl.BlockSpec((None, tm, LANES), lambda i, j: (i, j, 0)),
        ],
        out_specs=[
            pl.BlockSpec((None, bpt, PRIMARY_WIDTH // LANES, MOBA_BLOCK, LANES),
                         lambda i, j: (i, j, 0, 0, 0)),
            pl.BlockSpec((None, bpt, B_HEADS * VT_ROWS, MOBA_BLOCK), lambda i, j: (i, j, 0, 0)),
            pl.BlockSpec((None, bpt, 1, PRIMARY_WIDTH), lambda i, j: (i, j, 0, 0)),
        ],
        out_shape=[jax.ShapeDtypeStruct((b, nb, PRIMARY_WIDTH // LANES, MOBA_BLOCK, LANES), BF16),
                   jax.ShapeDtypeStruct((b, nb, B_HEADS * VT_ROWS, MOBA_BLOCK), BF16),
                   jax.ShapeDtypeStruct((b, nb, 1, PRIMARY_WIDTH), F32)],
        compiler_params=_params(("arbitrary", "arbitrary"), 48),
        name="shared_kv",
    )(h, g, w, kg, seg, cos, sin)


def _proj_b_body(h_ref, g_ref, win_ref, qg_ref, seg_ref, cst_ref,
                 kt_ref, v_ref, mqg_ref, mseg_ref, qt_ref, mem_ref):
    hn = _rms(h_ref[...], g_ref[...]).astype(BF16)
    u = jnp.dot(hn, win_ref[...], preferred_element_type=F32)
    qt = u[:, :PRIMARY_WIDTH].T
    ms = jnp.dot(seg_ref[...], (qt * qt).astype(BF16), preferred_element_type=F32)
    inv = lax.rsqrt(ms + EPS) * (SCALE * LOG2E)
    half = ROPE_DIM // 2
    cos = cst_ref[0:half, :]
    sin = cst_ref[half:ROPE_DIM, :]
    for h in range(B_HEADS):
        rows = slice(h * HEAD_DIM, (h + 1) * HEAD_DIM)
        qh = qt[rows] * inv[h:h + 1, :] * qg_ref[rows, :]
        x1, x2 = qh[:half], qh[half:ROPE_DIM]
        rot = jnp.concatenate([x1 * cos - x2 * sin, x2 * cos + x1 * sin], axis=0)
        qt_ref[h * HEAD_DIM:h * HEAD_DIM + ROPE_DIM, :] = rot.astype(BF16)
        qt_ref[h * HEAD_DIM + ROPE_DIM:(h + 1) * HEAD_DIM, :] = qh[ROPE_DIM:].astype(BF16)
    mem = _mem_attention(u[:, PRIMARY_WIDTH:], kt_ref[...], v_ref[...], mqg_ref[...], mseg_ref[...])
    mem_ref[...] = mem.astype(BF16)


def _proj_b(h, g, win, qg, seg, cst, kt, v, mqg, mseg, *, tm=512):
    b, s, d = h.shape
    m = kt.shape[2]
    const = lambda i, j: (0, 0)
    return pl.pallas_call(
        _proj_b_body,
        grid=(b, s // tm),
        in_specs=[
            pl.BlockSpec((None, tm, d), lambda i, j: (i, j, 0)),
            pl.BlockSpec((1, d), const),
            pl.BlockSpec(win.shape, const),
            pl.BlockSpec((PRIMARY_WIDTH, tm), const),
            pl.BlockSpec((2 * SUBLANES, PRIMARY_WIDTH), const),
            pl.BlockSpec((None, ROPE_DIM, tm), lambda i, j: (i, 0, j)),
            pl.BlockSpec((None, MEM_WIDTH, m), lambda i, j: (i, 0, 0)),
            pl.BlockSpec((None, m, MEM_WIDTH), lambda i, j: (i, 0, 0)),
            pl.BlockSpec((1, MEM_WIDTH), const),
            pl.BlockSpec((MEM_WIDTH, MEM_WIDTH), const),
        ],
        out_specs=[pl.BlockSpec((None, PRIMARY_WIDTH, tm), lambda i, j: (i, 0, j)),
                   pl.BlockSpec((None, tm, MEM_WIDTH), lambda i, j: (i, j, 0))],
        out_shape=[jax.ShapeDtypeStruct((b, PRIMARY_WIDTH, s), BF16),
                   jax.ShapeDtypeStruct((b, s, MEM_WIDTH), BF16)],
        compiler_params=_params(("arbitrary", "arbitrary"), 48),
        name="proj_b",
    )(h, g, win, qg, seg, cst, kt, v, mqg, mseg)


def _moba_body(ctl_ref, qt_ref, k_ref, vt_ref, km_ref, o_ref,
               bias_ref, qh_ref, s_ref, p_ref, m_ref, shift_ref, alpha_ref, acc_ref, out_ref):
    qi = pl.program_id(1)
    nb = km_ref.shape[0] // B_HEADS
    tq = qt_ref.shape[1]
    qt = qt_ref[...]

    def select_blocks():
        gate = jnp.dot(km_ref[...], qt, preferred_element_type=F32)
        blk = lax.broadcasted_iota(jnp.int32, (nb, tq), 0)
        for h in range(B_HEADS):
            g = gate[h * nb:(h + 1) * nb, :]
            rank = jnp.zeros((nb, tq), F32)
            for m in range(nb):
                gm = g[m:m + 1, :]
                valid = (m < qi).astype(F32)
                rank = rank + jnp.where(m < blk, jnp.where(gm >= g, valid, 0.0),
                                        jnp.where(gm > g, valid, 0.0))
            bias_ref[h * nb:(h + 1) * nb, :] = jnp.where(rank < MOBA_TOPK, 0.0, MASKED)

    pair_row = lax.broadcasted_iota(jnp.int32, (LANES, tq), 0)
    for h in range(B_HEADS):
        pair = h // 2
        in_head = (pair_row // HEAD_DIM) == (h % 2)
        qh_ref[h] = jnp.where(in_head, qt[pair * LANES:(pair + 1) * LANES, :],
                              jnp.zeros((LANES, tq), BF16))

    key_pos = lax.broadcasted_iota(jnp.int32, (MOBA_BLOCK, tq), 0)
    q_pos = lax.broadcasted_iota(jnp.int32, (MOBA_BLOCK, tq), 1)
    causal = key_pos <= q_pos

    def scores(kj, h, own):
        s = jnp.dot(k_ref[kj, h // 2], qh_ref[h], preferred_element_type=F32)
        return jnp.where(causal, s, MASKED) if own else s

    def head_rows(h):
        return pl.ds(h * VT_ROWS, VT_ROWS)

    def attend_fixed_shift(kj, own):
        for h in range(B_HEADS):
            shift = bound if own else bound - bias_ref[pl.ds(h * nb + kj, 1), :]
            p_ref[h] = jnp.exp2(scores(kj, h, own) - shift).astype(BF16)
        for h in range(B_HEADS):
            pv = jnp.dot(vt_ref[kj, head_rows(h), :], p_ref[h], preferred_element_type=F32)
            if own:
                acc_ref[head_rows(h), :] = pv
            else:
                acc_ref[head_rows(h), :] += pv

    def attend_running_max(kj, own):
        for h in range(B_HEADS):
            s_ref[h] = scores(kj, h, own)
        for h in range(B_HEADS):
            blk_max = jnp.max(s_ref[h], axis=0, keepdims=True)
            if own:
                m_ref[h:h + 1, :] = blk_max
                shift_ref[h:h + 1, :] = blk_max
            else:
                bias = bias_ref[pl.ds(h * nb + kj, 1), :]
                m_old = m_ref[h:h + 1, :]
                m_new = jnp.maximum(m_old, blk_max + bias)
                shift_ref[h:h + 1, :] = m_new - bias
                alpha_ref[h:h + 1, :] = jnp.exp2(m_old - m_new)
                m_ref[h:h + 1, :] = m_new
        for h in range(B_HEADS):
            p_ref[h] = jnp.exp2(s_ref[h] - shift_ref[h:h + 1, :]).astype(BF16)
        for h in range(B_HEADS):
            pv = jnp.dot(vt_ref[kj, head_rows(h), :], p_ref[h], preferred_element_type=F32)
            if own:
                acc_ref[head_rows(h), :] = pv
            else:
                acc_ref[head_rows(h), :] = alpha_ref[h:h + 1, :] * acc_ref[head_rows(h), :] + pv

    def attend_all(attend):
        attend(qi, True)
        select_blocks()

        def past_blocks(i, carry):
            for u in range(PAST_UNROLL):
                attend(PAST_UNROLL * i + u, False)
            return carry

        groups = lax.shift_right_logical(qi, PAST_UNROLL.bit_length() - 1)
        lax.fori_loop(0, groups, past_blocks, 0)
        done = groups * PAST_UNROLL
        piece = PAST_UNROLL // 2
        while piece:
            take = (qi & piece) != 0

            def leftover(done=done, piece=piece):
                for u in range(piece):
                    attend(done + u, False)

            pl.when(take)(leftover)
            done = done + jnp.where(take, piece, 0)
            piece //= 2

    bound = ctl_ref[0]
    use_fixed_shift = ctl_ref[1] > 0.5
    pl.when(use_fixed_shift)(lambda: attend_all(attend_fixed_shift))
    pl.when(jnp.logical_not(use_fixed_shift))(lambda: attend_all(attend_running_max))

    for h in range(B_HEADS):
        num = acc_ref[pl.ds(h * VT_ROWS, HEAD_DIM), :]
        den = acc_ref[pl.ds(h * VT_ROWS + HEAD_DIM, 1), :]
        out_ref[pl.ds(h * HEAD_DIM, HEAD_DIM), :] = num / den
    o_ref[...] = out_ref[...].T.astype(BF16)


def _moba(ctl, qt, kblk, vtblk, kmbd):
    b, w, s = qt.shape
    nb = s // MOBA_BLOCK
    stat = pltpu.VMEM((2 * SUBLANES, MOBA_BLOCK), F32)
    return pl.pallas_call(
        _moba_body,
        grid=(b, nb),
        in_specs=[
            pl.BlockSpec(memory_space=pltpu.SMEM),
            pl.BlockSpec((None, w, MOBA_BLOCK), lambda i, j: (i, 0, j)),
            pl.BlockSpec((None, nb, w // LANES, MOBA_BLOCK, LANES), lambda i, j: (i, 0, 0, 0, 0)),
            pl.BlockSpec((None, nb, B_HEADS * VT_ROWS, MOBA_BLOCK), lambda i, j: (i, 0, 0, 0)),
            pl.BlockSpec((None, B_HEADS * nb, w), lambda i, j: (i, 0, 0)),
        ],
        out_specs=pl.BlockSpec((None, MOBA_BLOCK, w), lambda i, j: (i, j, 0)),
        out_shape=jax.ShapeDtypeStruct((b, s, w), BF16),
        scratch_shapes=[pltpu.VMEM((B_HEADS * nb, MOBA_BLOCK), F32),
                        pltpu.VMEM((B_HEADS, LANES, MOBA_BLOCK), BF16),
                        pltpu.VMEM((B_HEADS, MOBA_BLOCK, MOBA_BLOCK), F32),
                        pltpu.VMEM((B_HEADS, MOBA_BLOCK, MOBA_BLOCK), BF16),
                        stat, stat, stat,
                        pltpu.VMEM((B_HEADS * VT_ROWS, MOBA_BLOCK), F32),
                        pltpu.VMEM((w, MOBA_BLOCK), F32)],
        compiler_params=_params(("arbitrary", "arbitrary"), 56),
        name="moba",
    )(ctl, qt, kblk, vtblk, kmbd)


def _row(v):
    return v.reshape(1, -1).astype(F32)


def _tiled_row(v, reps):
    return jnp.tile(v.astype(F32), reps).reshape(1, -1)


def kernel(x, mem, positions, ffn1_norm_g, ffn1_w_gate, ffn1_w_up, ffn1_w_down, mix_norm_g, mem_norm_g, w_mem_kv, mem_q_norm_g, mem_k_norm_g, w_o, ffn2_norm_g, ffn2_w_gate, ffn2_w_up, ffn2_w_down, a_w_in, a_dw_kernel, a_dw_bias, a_ln_g, a_ln_b, kv_norm_g, w_kv, k_norm_g, b_w_in, b_q_norm_g):
    b, s, d = x.shape
    t = b * s
    nb = s // MOBA_BLOCK
    seg_mem = _seg_mean_matrix(MEM_WIDTH)

    def ffn(h, norm_g, wg, wu, wd, layer, mix=None):
        out = _ffn(h.reshape(t, d), _row(norm_g[layer]), wg, wu, wd, layer, mix)
        return out.reshape(b, s, d)

    def memkv(layer):
        return _memkv(mem, _row(mem_norm_g[layer]), w_mem_kv[layer].astype(BF16),
                      _tiled_row(mem_k_norm_g[layer], MEM_HEADS), seg_mem)

    h = ffn(x, ffn1_norm_g, ffn1_w_gate, ffn1_w_up, ffn1_w_down, 0)
    kt0, v0 = memkv(0)
    dw = jnp.repeat(a_dw_kernel[0].reshape(CONV_WIDTH, CONV_CH), SUBLANES, axis=0)
    dw = dw.reshape(CONV_WIDTH * SUBLANES, CONV_CH // LANES, LANES).transpose(1, 0, 2)
    h = _mixer_a(h, _row(mix_norm_g[0]), a_w_in[0].astype(BF16), dw, _row(a_dw_bias[0]),
                 _row(a_ln_g[0]), _row(a_ln_b[0]), kt0, v0,
                 _tiled_row(mem_q_norm_g[0], MEM_HEADS), seg_mem, w_o[0].astype(BF16))
    h = ffn(h, ffn2_norm_g, ffn2_w_gate, ffn2_w_up, ffn2_w_down, 0)

    inv_freq = 1.0 / (ROPE_THETA ** (jnp.arange(0, ROPE_DIM, 2, dtype=F32) / ROPE_DIM))
    invf = jnp.concatenate([inv_freq, inv_freq]).reshape(ROPE_DIM, 1)
    cos, sin, cst = _rope_tables(positions.reshape(b, 1, s), invf)
    k, vt, km = _shared_kv(h, _row(kv_norm_g), w_kv.astype(BF16),
                           _tiled_row(k_norm_g, B_HEADS), seg_mem, cos, sin)
    kmh = km.reshape(b, nb, B_HEADS, HEAD_DIM).transpose(0, 2, 1, 3)
    eye = jnp.eye(B_HEADS, dtype=F32)
    kmbd = (kmh[:, :, :, None, :] * eye[None, :, None, :, None]).reshape(
        b, B_HEADS * nb, PRIMARY_WIDTH).astype(BF16)

    h = ffn(h, ffn1_norm_g, ffn1_w_gate, ffn1_w_up, ffn1_w_down, 1)
    kt1, v1 = memkv(1)
    proj_tm = 1024
    q_gain_t = jnp.broadcast_to(jnp.tile(b_q_norm_g[0].astype(F32), B_HEADS)[:, None],
                                (PRIMARY_WIDTH, proj_tm))
    head_of = np.arange(PRIMARY_WIDTH) // HEAD_DIM
    seg_rows = jnp.asarray((np.arange(2 * SUBLANES)[:, None] == head_of[None, :]) / HEAD_DIM, dtype=BF16)
    qt, mem_out = _proj_b(h, _row(mix_norm_g[1]), b_w_in[0].astype(BF16), q_gain_t, seg_rows, cst,
                          kt1, v1, _tiled_row(mem_q_norm_g[1], MEM_HEADS), seg_mem, tm=proj_tm)
    score_bound = (SCORE_BOUND_PER_GAIN * jnp.max(jnp.abs(b_q_norm_g[0]))
                   * jnp.max(jnp.abs(k_norm_g))).astype(F32)
    ctl = jnp.stack([score_bound, (score_bound <= MAX_FIXED_SHIFT).astype(F32)])
    prim = _moba(ctl, qt, k, vt, kmbd)
    mix = (prim.reshape(t, PRIMARY_WIDTH), mem_out.reshape(t, MEM_WIDTH), w_o[1].astype(BF16))
    return ffn(h, ffn2_norm_g, ffn2_w_gate, ffn2_w_up, ffn2_w_down, 1, mix)
```

```python
import functools

import numpy as np
import jax
import jax.numpy as jnp
from jax import lax
from jax.experimental import pallas as pl
from jax.experimental.pallas import tpu as pltpu

F32 = jnp.float32
BF16 = jnp.bfloat16

D_MODEL = 1024
HEAD_DIM = 64
MEM_HEADS = 4
MEM_WIDTH = MEM_HEADS * HEAD_DIM
PRIMARY_WIDTH = D_MODEL - MEM_WIDTH
B_HEADS = PRIMARY_WIDTH // HEAD_DIM
CONV_CH = PRIMARY_WIDTH
CONV_WIDTH = 31
MOBA_BLOCK = 256
MOBA_TOPK = 3
ROPE_THETA = 500000.0
ROPE_DIM = HEAD_DIM // 4
EPS = 1e-6
SCALE = HEAD_DIM ** -0.5
LOG2E = float(np.log2(np.e))

LANES = 128
SUBLANES = 8
HALO = 32
CONV_ROWS = 64
FFN_CHUNK = 256
MASKED = -1e30
VT_ROWS = HEAD_DIM + 16
SCORE_BOUND_PER_GAIN = 1.05 * HEAD_DIM * SCALE * LOG2E
MAX_FIXED_SHIFT = 60.0
PAST_UNROLL = 4
MIB = 1024 * 1024


def _params(semantics, vmem_mib):
    return pltpu.CompilerParams(dimension_semantics=semantics,
                                vmem_limit_bytes=vmem_mib * MIB)


def _rms(x, g):
    ms = jnp.mean(x * x, axis=-1, keepdims=True)
    return x * lax.rsqrt(ms + EPS) * g


def _head_rms(x, g, seg_mean):
    ms = jnp.dot((x * x).astype(BF16), seg_mean, preferred_element_type=F32)
    return x * lax.rsqrt(ms + EPS) * g


def _seg_mean_matrix(width):
    idx = np.arange(width) // HEAD_DIM
    return jnp.asarray((idx[:, None] == idx[None, :]).astype(np.float32) / HEAD_DIM, dtype=BF16)


def _rope(x, cos, sin):
    lane = lax.broadcasted_iota(jnp.int32, (1, LANES), 1) % HEAD_DIM
    first_half = lane < (ROPE_DIM // 2)
    outs = []
    for c in range(x.shape[1] // LANES):
        xc = x[:, c * LANES:(c + 1) * LANES]
        partner = jnp.where(first_half,
                            pltpu.roll(xc, LANES - ROPE_DIM // 2, 1),
                            pltpu.roll(xc, ROPE_DIM // 2, 1))
        outs.append(xc * cos + partner * sin)
    return jnp.concatenate(outs, axis=1)


def _mem_attention(qm, kt, v, qg, seg_mean):
    qn = _head_rms(qm, qg, seg_mean) * SCALE
    lane_head = lax.broadcasted_iota(jnp.int32, (1, MEM_WIDTH), 1) // HEAD_DIM
    out = jnp.zeros(qm.shape, F32)
    for h in range(MEM_HEADS):
        qh = jnp.where(lane_head == h, qn, 0.0).astype(BF16)
        s = jnp.dot(qh, kt, preferred_element_type=F32)
        m = jnp.max(s, axis=-1, keepdims=True)
        p = jnp.exp(s - m)
        l = jnp.sum(p, axis=-1, keepdims=True)
        vh = jnp.where(lane_head == h, v, jnp.zeros_like(v))
        out = out + jnp.dot(p.astype(BF16), vh, preferred_element_type=F32) / l
    return out


def _ffn_body(has_mix, *refs):
    if has_mix:
        prim_ref, mem_ref, wo_ref, *refs = refs
    x_ref, g_ref, wg_ref, wu_ref, wd_ref, o_ref, hmid_ref = refs
    x = x_ref[...]
    if has_mix:
        x = (x + jnp.dot(prim_ref[...], wo_ref[0:PRIMARY_WIDTH, :], preferred_element_type=F32)
             + jnp.dot(mem_ref[...], wo_ref[PRIMARY_WIDTH:, :], preferred_element_type=F32))
    xn = _rms(x, g_ref[...]).astype(BF16)
    for c in range(wg_ref.shape[1] // FFN_CHUNK):
        cols = pl.ds(c * FFN_CHUNK, FFN_CHUNK)
        gate = jnp.dot(xn, wg_ref[:, cols], preferred_element_type=F32)
        up = jnp.dot(xn, wu_ref[:, cols], preferred_element_type=F32)
        hmid_ref[:, cols] = (gate * jax.nn.sigmoid(gate) * up).astype(BF16)
    o_ref[...] = x + 0.5 * jnp.dot(hmid_ref[...], wd_ref[...], preferred_element_type=F32)


def _ffn(h2d, g, wg, wu, wd, layer, mix=None, *, tm=512):
    t, d = h2d.shape
    f = wg.shape[2]
    resident = dict(pipeline_mode=pl.Buffered(1))
    mix_args, mix_specs = (), []
    if mix is not None:
        mix_args = mix
        mix_specs = [
            pl.BlockSpec((tm, PRIMARY_WIDTH), lambda i: (i, 0)),
            pl.BlockSpec((tm, MEM_WIDTH), lambda i: (i, 0)),
            pl.BlockSpec((d, d), lambda i: (0, 0), **resident),
        ]
    return pl.pallas_call(
        functools.partial(_ffn_body, mix is not None),
        grid=(t // tm,),
        in_specs=mix_specs + [
            pl.BlockSpec((tm, d), lambda i: (i, 0)),
            pl.BlockSpec((1, d), lambda i: (0, 0)),
            pl.BlockSpec((None, d, f), lambda i: (layer, 0, 0), **resident),
            pl.BlockSpec((None, d, f), lambda i: (layer, 0, 0), **resident),
            pl.BlockSpec((None, f, d), lambda i: (layer, 0, 0), **resident),
        ],
        out_specs=pl.BlockSpec((tm, d), lambda i: (i, 0)),
        out_shape=jax.ShapeDtypeStruct((t, d), F32),
        scratch_shapes=[pltpu.VMEM((tm, f), BF16)],
        compiler_params=_params(("arbitrary",), 56),
        name="ffn_mix" if mix is not None else "ffn",
    )(*mix_args, h2d, g, wg, wu, wd)


def _memkv_body(mem_ref, g_ref, w_ref, kg_ref, seg_ref, kt_ref, v_ref):
    mn = _rms(mem_ref[...], g_ref[...]).astype(BF16)
    kv = jnp.dot(mn, w_ref[...], preferred_element_type=F32)
    k = _head_rms(kv[:, :MEM_WIDTH], kg_ref[...], seg_ref[...])
    kt_ref[...] = k.T.astype(BF16)
    v_ref[...] = kv[:, MEM_WIDTH:].astype(BF16)


def _memkv(mem, g, w, kg, seg):
    b, m, d = mem.shape
    return pl.pallas_call(
        _memkv_body,
        grid=(b,),
        in_specs=[
            pl.BlockSpec((None, m, d), lambda i: (i, 0, 0)),
            pl.BlockSpec((1, d), lambda i: (0, 0)),
            pl.BlockSpec((d, 2 * MEM_WIDTH), lambda i: (0, 0)),
            pl.BlockSpec((1, MEM_WIDTH), lambda i: (0, 0)),
            pl.BlockSpec((MEM_WIDTH, MEM_WIDTH), lambda i: (0, 0)),
        ],
        out_specs=[
            pl.BlockSpec((None, MEM_WIDTH, m), lambda i: (i, 0, 0)),
            pl.BlockSpec((None, m, MEM_WIDTH), lambda i: (i, 0, 0)),
        ],
        out_shape=[jax.ShapeDtypeStruct((b, MEM_WIDTH, m), BF16),
                   jax.ShapeDtypeStruct((b, m, MEM_WIDTH), BF16)],
        compiler_params=_params(("arbitrary",), 32),
        name="memkv",
    )(mem, g, w, kg, seg)


def _mixer_a_body(tm, h_ref, g_ref, win_ref, dw_ref, db_ref, lng_ref, lnb_ref,
                  kt_ref, v_ref, qg_ref, seg_ref, wo_ref, o_ref, buf_ref, conv_ref):
    n_strips = CONV_CH // LANES

    @pl.when(pl.program_id(1) == 0)
    def _():
        buf_ref[0, :, 0:HALO, :] = jnp.zeros((n_strips, HALO, LANES), F32)

    h = h_ref[...]
    hn = _rms(h, g_ref[...]).astype(BF16)
    u = jnp.dot(hn, win_ref[...], preferred_element_type=F32)
    glu = u[:, :CONV_CH] * jax.nn.sigmoid(u[:, CONV_CH:2 * CONV_CH])

    groups = CONV_ROWS // SUBLANES
    for cb in range(n_strips):
        buf_ref[0, cb, HALO:HALO + tm, :] = glu[:, cb * LANES:(cb + 1) * LANES]
        x_strip = buf_ref[0, cb]
        for r in range(1, SUBLANES):
            buf_ref[r, cb] = pltpu.roll(x_strip, r, 0)

        def conv_chunk(c, carry, cb=cb):
            base = pl.multiple_of(c * CONV_ROWS, CONV_ROWS)
            accs = [jnp.zeros((SUBLANES, LANES), F32)] * groups
            for j in range(CONV_WIDTH):
                k = CONV_WIDTH - 1 - j
                start = base + (HALO - SUBLANES * (j // SUBLANES))
                w8 = dw_ref[cb, k * SUBLANES:(k + 1) * SUBLANES, :]
                xs = buf_ref[j % SUBLANES, cb, pl.ds(start, CONV_ROWS), :]
                accs = [accs[i] + w8 * xs[i * SUBLANES:(i + 1) * SUBLANES] for i in range(groups)]
            conv_ref[pl.ds(base, CONV_ROWS), pl.ds(cb * LANES, LANES)] = jnp.concatenate(accs, axis=0)
            return carry

        lax.fori_loop(0, tm // CONV_ROWS, conv_chunk, 0, unroll=2)
    buf_ref[0, :, 0:HALO, :] = buf_ref[0, :, tm:tm + HALO, :]

    c = conv_ref[...] + db_ref[...]
    mu = jnp.mean(c, axis=-1, keepdims=True)
    xc = c - mu
    var = jnp.mean(xc * xc, axis=-1, keepdims=True)
    y = xc * lax.rsqrt(var + EPS) * lng_ref[...] + lnb_ref[...]
    prim = (y * jax.nn.sigmoid(y)).astype(BF16)

    mem = _mem_attention(u[:, 2 * CONV_CH:], kt_ref[...], v_ref[...], qg_ref[...], seg_ref[...])
    o_ref[...] = (h
                  + jnp.dot(prim, wo_ref[0:CONV_CH, :], preferred_element_type=F32)
                  + jnp.dot(mem.astype(BF16), wo_ref[CONV_CH:, :], preferred_element_type=F32))


def _mixer_a(h, g, win, dw, db, lng, lnb, kt, v, qg, seg, wo, *, tm=512):
    b, s, d = h.shape
    m = kt.shape[2]
    const = lambda i, j: (0, 0)
    return pl.pallas_call(
        functools.partial(_mixer_a_body, tm),
        grid=(b, s // tm),
        in_specs=[
            pl.BlockSpec((None, tm, d), lambda i, j: (i, j, 0)),
            pl.BlockSpec((1, d), const),
            pl.BlockSpec(win.shape, const),
            pl.BlockSpec(dw.shape, lambda i, j: (0, 0, 0)),
            pl.BlockSpec((1, CONV_CH), const),
            pl.BlockSpec((1, CONV_CH), const),
            pl.BlockSpec((1, CONV_CH), const),
            pl.BlockSpec((None, MEM_WIDTH, m), lambda i, j: (i, 0, 0)),
            pl.BlockSpec((None, m, MEM_WIDTH), lambda i, j: (i, 0, 0)),
            pl.BlockSpec((1, MEM_WIDTH), const),
            pl.BlockSpec((MEM_WIDTH, MEM_WIDTH), const),
            pl.BlockSpec((d, d), const),
        ],
        out_specs=pl.BlockSpec((None, tm, d), lambda i, j: (i, j, 0)),
        out_shape=jax.ShapeDtypeStruct((b, s, d), F32),
        scratch_shapes=[pltpu.VMEM((SUBLANES, CONV_CH // LANES, tm + HALO, LANES), F32),
                        pltpu.VMEM((tm, CONV_CH), F32)],
        compiler_params=_params(("arbitrary", "arbitrary"), 48),
        name="mixer_a",
    )(h, g, win, dw, db, lng, lnb, kt, v, qg, seg, wo)


def _rope_tab_body(pos_ref, invf_ref, cos_ref, sin_ref, cst_ref):
    tm = pos_ref.shape[1]
    ang = invf_ref[...] * pos_ref[...].astype(F32)
    c = jnp.cos(ang)
    s = jnp.sin(ang)
    half = ROPE_DIM // 2
    cst_ref[...] = jnp.concatenate([c[:half], s[:half]], axis=0)
    row = lax.broadcasted_iota(jnp.int32, ang.shape, 0)
    s = jnp.where(row < ROPE_DIM // 2, -s, s)
    rest = HEAD_DIM - ROPE_DIM
    ones = jnp.ones((rest, tm), F32)
    zeros = jnp.zeros((rest, tm), F32)
    cos_ref[...] = jnp.concatenate([c, ones, c, ones], axis=0).T
    sin_ref[...] = jnp.concatenate([s, zeros, s, zeros], axis=0).T


def _rope_tables(pos3, invf, *, tm=512):
    b, _, s = pos3.shape
    return pl.pallas_call(
        _rope_tab_body,
        grid=(b, s // tm),
        in_specs=[
            pl.BlockSpec((None, 1, tm), lambda i, j: (i, 0, j)),
            pl.BlockSpec((ROPE_DIM, 1), lambda i, j: (0, 0)),
        ],
        out_specs=[pl.BlockSpec((None, tm, LANES), lambda i, j: (i, j, 0)),
                   pl.BlockSpec((None, tm, LANES), lambda i, j: (i, j, 0)),
                   pl.BlockSpec((None, ROPE_DIM, tm), lambda i, j: (i, 0, j))],
        out_shape=[jax.ShapeDtypeStruct((b, s, LANES), F32),
                   jax.ShapeDtypeStruct((b, s, LANES), F32),
                   jax.ShapeDtypeStruct((b, ROPE_DIM, s), F32)],
        compiler_params=_params(("arbitrary", "arbitrary"), 32),
        name="rope_tab",
    )(pos3, invf)


def _shared_kv_body(tm, h_ref, g_ref, w_ref, kg_ref, seg_ref, cos_ref, sin_ref,
                    k_ref, vt_ref, km_ref):
    hn = _rms(h_ref[...], g_ref[...]).astype(BF16)
    k = jnp.dot(hn, w_ref[:, 0:PRIMARY_WIDTH], preferred_element_type=F32)
    v = jnp.dot(hn, w_ref[:, PRIMARY_WIDTH:], preferred_element_type=F32)
    for c in range(PRIMARY_WIDTH // FFN_CHUNK):
        cols = pl.ds(c * FFN_CHUNK, FFN_CHUNK)
        kc = k[:, c * FFN_CHUNK:(c + 1) * FFN_CHUNK]
        kr = _rope(_head_rms(kc, kg_ref[:, cols], seg_ref[...]), cos_ref[...], sin_ref[...])
        kb = kr.astype(BF16)
        for blk in range(tm // MOBA_BLOCK):
            rows = slice(blk * MOBA_BLOCK, (blk + 1) * MOBA_BLOCK)
            km_ref[blk, :, cols] = jnp.mean(kr[rows], axis=0, keepdims=True)
            for pp in range(FFN_CHUNK // LANES):
                k_ref[blk, c * (FFN_CHUNK // LANES) + pp] = kb[rows, pp * LANES:(pp + 1) * LANES]
    ones = jnp.ones((VT_ROWS - HEAD_DIM, MOBA_BLOCK), BF16)
    for blk in range(tm // MOBA_BLOCK):
        rows = slice(blk * MOBA_BLOCK, (blk + 1) * MOBA_BLOCK)
        vt = v[rows].T.astype(BF16)
        for h in range(B_HEADS):
            vt_ref[blk, h * VT_ROWS:h * VT_ROWS + HEAD_DIM, :] = vt[h * HEAD_DIM:(h + 1) * HEAD_DIM]
            vt_ref[blk, h * VT_ROWS + HEAD_DIM:(h + 1) * VT_ROWS, :] = ones


def _shared_kv(h, g, w, kg, seg, cos, sin, *, tm=1024):
    b, s, d = h.shape
    nb = s // MOBA_BLOCK
    bpt = tm // MOBA_BLOCK
    const = lambda i, j: (0, 0)
    return pl.pallas_call(
        functools.partial(_shared_kv_body, tm),
        grid=(b, s // tm),
        in_specs=[
            pl.BlockSpec((None, tm, d), lambda i, j: (i, j, 0)),
            pl.BlockSpec((1, d), const),
            pl.BlockSpec(w.shape, const),
            pl.BlockSpec((1, PRIMARY_WIDTH), const),
            pl.BlockSpec((FFN_CHUNK, FFN_CHUNK), const),
            pl.BlockSpec((None, tm, LANES), lambda i, j: (i, j, 0)),
            pl.BlockSpec((None, tm, LANES), lambda i, j: (i, j, 0)),
        ],
        out_specs=[
            pl.BlockSpec((None, bpt, PRIMARY_WIDTH // LANES, MOBA_BLOCK, LANES),
                         lambda i, j: (i, j, 0, 0, 0)),
            pl.BlockSpec((None, bpt, B_HEADS * VT_ROWS, MOBA_BLOCK), lambda i, j: (i, j, 0, 0)),
            pl.BlockSpec((None, bpt, 1, PRIMARY_WIDTH), lambda i, j: (i, j, 0, 0)),
        ],
        out_shape=[jax.ShapeDtypeStruct((b, nb, PRIMARY_WIDTH // LANES, MOBA_BLOCK, LANES), BF16),
                   jax.ShapeDtypeStruct((b, nb, B_HEADS * VT_ROWS, MOBA_BLOCK), BF16),
                   jax.ShapeDtypeStruct((b, nb, 1, PRIMARY_WIDTH), F32)],
        compiler_params=_params(("arbitrary", "arbitrary"), 48),
        name="shared_kv",
    )(h, g, w, kg, seg, cos, sin)


def _proj_b_body(h_ref, g_ref, win_ref, qg_ref, seg_ref, cst_ref,
                 kt_ref, v_ref, mqg_ref, mseg_ref, qt_ref, mem_ref):
    hn = _rms(h_ref[...], g_ref[...]).astype(BF16)
    u = jnp.dot(hn, win_ref[...], preferred_element_type=F32)
    qt = u[:, :PRIMARY_WIDTH].T
    ms = jnp.dot(seg_ref[...], (qt * qt).astype(BF16), preferred_element_type=F32)
    inv = lax.rsqrt(ms + EPS) * (SCALE * LOG2E)
    half = ROPE_DIM // 2
    cos = cst_ref[0:half, :]
    sin = cst_ref[half:ROPE_DIM, :]
    for h in range(B_HEADS):
        rows = slice(h * HEAD_DIM, (h + 1) * HEAD_DIM)
        qh = qt[rows] * inv[h:h + 1, :] * qg_ref[rows, :]
        x1, x2 = qh[:half], qh[half:ROPE_DIM]
        rot = jnp.concatenate([x1 * cos - x2 * sin, x2 * cos + x1 * sin], axis=0)
        qt_ref[h * HEAD_DIM:h * HEAD_DIM + ROPE_DIM, :] = rot.astype(BF16)
        qt_ref[h * HEAD_DIM + ROPE_DIM:(h + 1) * HEAD_DIM, :] = qh[ROPE_DIM:].astype(BF16)
    mem = _mem_attention(u[:, PRIMARY_WIDTH:], kt_ref[...], v_ref[...], mqg_ref[...], mseg_ref[...])
    mem_ref[...] = mem.astype(BF16)


def _proj_b(h, g, win, qg, seg, cst, kt, v, mqg, mseg, *, tm=512):
    b, s, d = h.shape
    m = kt.shape[2]
    const = lambda i, j: (0, 0)
    return pl.pallas_call(
        _proj_b_body,
        grid=(b, s // tm),
        in_specs=[
            pl.BlockSpec((None, tm, d), lambda i, j: (i, j, 0)),
            pl.BlockSpec((1, d), const),
            pl.BlockSpec(win.shape, const),
            pl.BlockSpec((PRIMARY_WIDTH, tm), const),
            pl.BlockSpec((2 * SUBLANES, PRIMARY_WIDTH), const),
            pl.BlockSpec((None, ROPE_DIM, tm), lambda i, j: (i, 0, j)),
            pl.BlockSpec((None, MEM_WIDTH, m), lambda i, j: (i, 0, 0)),
            pl.BlockSpec((None, m, MEM_WIDTH), lambda i, j: (i, 0, 0)),
            pl.BlockSpec((1, MEM_WIDTH), const),
            pl.BlockSpec((MEM_WIDTH, MEM_WIDTH), const),
        ],
        out_specs=[pl.BlockSpec((None, PRIMARY_WIDTH, tm), lambda i, j: (i, 0, j)),
                   pl.BlockSpec((None, tm, MEM_WIDTH), lambda i, j: (i, j, 0))],
        out_shape=[jax.ShapeDtypeStruct((b, PRIMARY_WIDTH, s), BF16),
                   jax.ShapeDtypeStruct((b, s, MEM_WIDTH), BF16)],
        compiler_params=_params(("arbitrary", "arbitrary"), 48),
        name="proj_b",
    )(h, g, win, qg, seg, cst, kt, v, mqg, mseg)


def _moba_body(ctl_ref, qt_ref, k_ref, vt_ref, km_ref, o_ref,
               bias_ref, qh_ref, s_ref, p_ref, m_ref, shift_ref, alpha_ref, acc_ref, out_ref):
    qi = pl.program_id(1)
    nb = km_ref.shape[0] // B_HEADS
    tq = qt_ref.shape[1]
    qt = qt_ref[...]

    def select_blocks():
        gate = jnp.dot(km_ref[...], qt, preferred_element_type=F32)
        blk = lax.broadcasted_iota(jnp.int32, (nb, tq), 0)
        for h in range(B_HEADS):
            g = gate[h * nb:(h + 1) * nb, :]
            rank = jnp.zeros((nb, tq), F32)
            for m in range(nb):
                gm = g[m:m + 1, :]
                valid = (m < qi).astype(F32)
                rank = rank + jnp.where(m < blk, jnp.where(gm >= g, valid, 0.0),
                                        jnp.where(gm > g, valid, 0.0))
            bias_ref[h * nb:(h + 1) * nb, :] = jnp.where(rank < MOBA_TOPK, 0.0, MASKED)

    pair_row = lax.broadcasted_iota(jnp.int32, (LANES, tq), 0)
    for h in range(B_HEADS):
        pair = h // 2
        in_head = (pair_row // HEAD_DIM) == (h % 2)
        qh_ref[h] = jnp.where(in_head, qt[pair * LANES:(pair + 1) * LANES, :],
                              jnp.zeros((LANES, tq), BF16))

    key_pos = lax.broadcasted_iota(jnp.int32, (MOBA_BLOCK, tq), 0)
    q_pos = lax.broadcasted_iota(jnp.int32, (MOBA_BLOCK, tq), 1)
    causal = key_pos <= q_pos

    def scores(kj, h, own):
        s = jnp.dot(k_ref[kj, h // 2], qh_ref[h], preferred_element_type=F32)
        return jnp.where(causal, s, MASKED) if own else s

    def head_rows(h):
        return pl.ds(h * VT_ROWS, VT_ROWS)

    def attend_fixed_shift(kj, own):
        for h in range(B_HEADS):
            shift = bound if own else bound - bias_ref[pl.ds(h * nb + kj, 1), :]
            p_ref[h] = jnp.exp2(scores(kj, h, own) - shift).astype(BF16)
        for h in range(B_HEADS):
            pv = jnp.dot(vt_ref[kj, head_rows(h), :], p_ref[h], preferred_element_type=F32)
            if own:
                acc_ref[head_rows(h), :] = pv
            else:
                acc_ref[head_rows(h), :] += pv

    def attend_running_max(kj, own):
        for h in range(B_HEADS):
            s_ref[h] = scores(kj, h, own)
        for h in range(B_HEADS):
            blk_max = jnp.max(s_ref[h], axis=0, keepdims=True)
            if own:
                m_ref[h:h + 1, :] = blk_max
                shift_ref[h:h + 1, :] = blk_max
            else:
                bias = bias_ref[pl.ds(h * nb + kj, 1), :]
                m_old = m_ref[h:h + 1, :]
                m_new = jnp.maximum(m_old, blk_max + bias)
                shift_ref[h:h + 1, :] = m_new - bias
                alpha_ref[h:h + 1, :] = jnp.exp2(m_old - m_new)
                m_ref[h:h + 1, :] = m_new
        for h in range(B_HEADS):
            p_ref[h] = jnp.exp2(s_ref[h] - shift_ref[h:h + 1, :]).astype(BF16)
        for h in range(B_HEADS):
            pv = jnp.dot(vt_ref[kj, head_rows(h), :], p_ref[h], preferred_element_type=F32)
            if own:
                acc_ref[head_rows(h), :] = pv
            else:
                acc_ref[head_rows(h), :] = alpha_ref[h:h + 1, :] * acc_ref[head_rows(h), :] + pv

    def attend_all(attend):
        attend(qi, True)
        select_blocks()

        def past_blocks(i, carry):
            for u in range(PAST_UNROLL):
                attend(PAST_UNROLL * i + u, False)
            return carry

        groups = lax.shift_right_logical(qi, PAST_UNROLL.bit_length() - 1)
        lax.fori_loop(0, groups, past_blocks, 0)
        done = groups * PAST_UNROLL
        piece = PAST_UNROLL // 2
        while piece:
            take = (qi & piece) != 0

            def leftover(done=done, piece=piece):
                for u in range(piece):
                    attend(done + u, False)

            pl.when(take)(leftover)
            done = done + jnp.where(take, piece, 0)
            piece //= 2

    bound = ctl_ref[0]
    use_fixed_shift = ctl_ref[1] > 0.5
    pl.when(use_fixed_shift)(lambda: attend_all(attend_fixed_shift))
    pl.when(jnp.logical_not(use_fixed_shift))(lambda: attend_all(attend_running_max))

    for h in range(B_HEADS):
        num = acc_ref[pl.ds(h * VT_ROWS, HEAD_DIM), :]
        den = acc_ref[pl.ds(h * VT_ROWS + HEAD_DIM, 1), :]
        out_ref[pl.ds(h * HEAD_DIM, HEAD_DIM), :] = num / den
    o_ref[...] = out_ref[...].T.astype(BF16)


def _moba(ctl, qt, kblk, vtblk, kmbd):
    b, w, s = qt.shape
    nb = s // MOBA_BLOCK
    stat = pltpu.VMEM((2 * SUBLANES, MOBA_BLOCK), F32)
    return pl.pallas_call(
        _moba_body,
        grid=(b, nb),
        in_specs=[
            pl.BlockSpec(memory_space=pltpu.SMEM),
            pl.BlockSpec((None, w, MOBA_BLOCK), lambda i, j: (i, 0, j)),
            pl.BlockSpec((None, nb, w // LANES, MOBA_BLOCK, LANES), lambda i, j: (i, 0, 0, 0, 0)),
            pl.BlockSpec((None, nb, B_HEADS * VT_ROWS, MOBA_BLOCK), lambda i, j: (i, 0, 0, 0)),
            pl.BlockSpec((None, B_HEADS * nb, w), lambda i, j: (i, 0, 0)),
        ],
        out_specs=pl.BlockSpec((None, MOBA_BLOCK, w), lambda i, j: (i, j, 0)),
        out_shape=jax.ShapeDtypeStruct((b, s, w), BF16),
        scratch_shapes=[pltpu.VMEM((B_HEADS * nb, MOBA_BLOCK), F32),
                        pltpu.VMEM((B_HEADS, LANES, MOBA_BLOCK), BF16),
                        pltpu.VMEM((B_HEADS, MOBA_BLOCK, MOBA_BLOCK), F32),
                        pltpu.VMEM((B_HEADS, MOBA_BLOCK, MOBA_BLOCK), BF16),
                        stat, stat, stat,
                        pltpu.VMEM((B_HEADS * VT_ROWS, MOBA_BLOCK), F32),
                        pltpu.VMEM((w, MOBA_BLOCK), F32)],
        compiler_params=_params(("arbitrary", "arbitrary"), 56),
        name="moba",
    )(ctl, qt, kblk, vtblk, kmbd)


def _row(v):
    return v.reshape(1, -1).astype(F32)


def _tiled_row(v, reps):
    return jnp.tile(v.astype(F32), reps).reshape(1, -1)


def kernel(x, mem, positions, ffn1_norm_g, ffn1_w_gate, ffn1_w_up, ffn1_w_down, mix_norm_g, mem_norm_g, w_mem_kv, mem_q_norm_g, mem_k_norm_g, w_o, ffn2_norm_g, ffn2_w_gate, ffn2_w_up, ffn2_w_down, a_w_in, a_dw_kernel, a_dw_bias, a_ln_g, a_ln_b, kv_norm_g, w_kv, k_norm_g, b_w_in, b_q_norm_g):
    b, s, d = x.shape
    t = b * s
    nb = s // MOBA_BLOCK
    seg_mem = _seg_mean_matrix(MEM_WIDTH)

    def ffn(h, norm_g, wg, wu, wd, layer, mix=None):
        out = _ffn(h.reshape(t, d), _row(norm_g[layer]), wg, wu, wd, layer, mix)
        return out.reshape(b, s, d)

    def memkv(layer):
        return _memkv(mem, _row(mem_norm_g[layer]), w_mem_kv[layer].astype(BF16),
                      _tiled_row(mem_k_norm_g[layer], MEM_HEADS), seg_mem)

    h = ffn(x, ffn1_norm_g, ffn1_w_gate, ffn1_w_up, ffn1_w_down, 0)
    kt0, v0 = memkv(0)
    dw = jnp.repeat(a_dw_kernel[0].reshape(CONV_WIDTH, CONV_CH), SUBLANES, axis=0)
    dw = dw.reshape(CONV_WIDTH * SUBLANES, CONV_CH // LANES, LANES).transpose(1, 0, 2)
    h = _mixer_a(h, _row(mix_norm_g[0]), a_w_in[0].astype(BF16), dw, _row(a_dw_bias[0]),
                 _row(a_ln_g[0]), _row(a_ln_b[0]), kt0, v0,
                 _tiled_row(mem_q_norm_g[0], MEM_HEADS), seg_mem, w_o[0].astype(BF16))
    h = ffn(h, ffn2_norm_g, ffn2_w_gate, ffn2_w_up, ffn2_w_down, 0)

    inv_freq = 1.0 / (ROPE_THETA ** (jnp.arange(0, ROPE_DIM, 2, dtype=F32) / ROPE_DIM))
    invf = jnp.concatenate([inv_freq, inv_freq]).reshape(ROPE_DIM, 1)
    cos, sin, cst = _rope_tables(positions.reshape(b, 1, s), invf)
    k, vt, km = _shared_kv(h, _row(kv_norm_g), w_kv.astype(BF16),
                           _tiled_row(k_norm_g, B_HEADS), seg_mem, cos, sin)
    kmh = km.reshape(b, nb, B_HEADS, HEAD_DIM).transpose(0, 2, 1, 3)
    eye = jnp.eye(B_HEADS, dtype=F32)
    kmbd = (kmh[:, :, :, None, :] * eye[None, :, None, :, None]).reshape(
        b, B_HEADS * nb, PRIMARY_WIDTH).astype(BF16)

    h = ffn(h, ffn1_norm_g, ffn1_w_gate, ffn1_w_up, ffn1_w_down, 1)
    kt1, v1 = memkv(1)
    proj_tm = 1024
    q_gain_t = jnp.broadcast_to(jnp.tile(b_q_norm_g[0].astype(F32), B_HEADS)[:, None],
                                (PRIMARY_WIDTH, proj_tm))
    head_of = np.arange(PRIMARY_WIDTH) // HEAD_DIM
    seg_rows = jnp.asarray((np.arange(2 * SUBLANES)[:, None] == head_of[None, :]) / HEAD_DIM, dtype=BF16)
    qt, mem_out = _proj_b(h, _row(mix_norm_g[1]), b_w_in[0].astype(BF16), q_gain_t, seg_rows, cst,
                          kt1, v1, _tiled_row(mem_q_norm_g[1], MEM_HEADS), seg_mem, tm=proj_tm)
    score_bound = (SCORE_BOUND_PER_GAIN * jnp.max(jnp.abs(b_q_norm_g[0]))
                   * jnp.max(jnp.abs(k_norm_g))).astype(F32)
    ctl = jnp.stack([score_bound, (score_bound <= MAX_FIXED_SHIFT).astype(F32)])
    prim = _moba(ctl, qt, k, vt, kmbd)
    mix = (prim.reshape(t, PRIMARY_WIDTH), mem_out.reshape(t, MEM_WIDTH), w_o[1].astype(BF16))
    return ffn(h, ffn2_norm_g, ffn2_w_gate, ffn2_w_up, ffn2_w_down, 1, mix)
```
